```python
import jax, jax.numpy as jnp
from jax import lax
import numpy as np

D_MODEL = 1024
BATCH = 4
SEQ = 8192
DEPTH = 1
DEC_BATCH = 32
DEC_SEQ = 1
PAST_LEN = 16384
PAGE_SIZE = 128

N_HEADS = 8
HEAD_DIM = 64
N_KV = 2
GROUP = N_HEADS // N_KV
CMP_LEN = 32
CMP_STRIDE = 16
CMP_SUB = CMP_LEN // CMP_STRIDE
CMP_HIDDEN = 128
SEL_BLOCK = 64
CMP_RATIO = SEL_BLOCK // CMP_STRIDE
CMP_OVER = CMP_SUB - 1
SEL_TOP = 16
WINDOW = 512
Q_BLOCK = 128
FORCE_SCORE = 1e4
D_B = 512
N_GROUPS_B = 4
GROUP_CH = D_B // N_GROUPS_B
CHUNK = 128
D_FF = 2816
ALPHA = (2.0 * DEPTH) ** 0.25
BETA = (8.0 * DEPTH) ** -0.25
LN_EPS = 1e-5
NEG_INF = -1e30
Q_W = N_HEADS * HEAD_DIM
KV_W = N_KV * HEAD_DIM
GATE_W = N_HEADS * 3
IN_W = Q_W + 6 * KV_W + GATE_W + 2 * D_B + 2 * D_MODEL

kernel_name = 'nsa_gmlp_macaron_deepnorm_step'


def layer_norm(x, g, b):
    xf = x.astype(jnp.float32)
    mu = jnp.mean(xf, axis=-1, keepdims=True)
    var = jnp.mean(jnp.square(xf - mu), axis=-1, keepdims=True)
    return ((xf - mu) * lax.rsqrt(var + LN_EPS) * g + b).astype(x.dtype)


def swiglu(x, w_in, w_out):
    gate, up = jnp.split(x @ w_in, 2, axis=-1)
    return (jax.nn.silu(gate) * up) @ w_out


def half_ffn_block(x, w_in, w_out, g, b):
    return layer_norm(ALPHA * x + 0.5 * swiglu(x, w_in, w_out), g, b)


def alibi_slopes():
    h = jnp.arange(1, N_HEADS + 1, dtype=jnp.float32)
    return jnp.exp2(-8.0 * h / N_HEADS).reshape(N_KV, GROUP)


def masked_softmax(s, mask):
    s = jnp.where(mask, s.astype(jnp.float32), NEG_INF)
    return jnp.where(mask, jax.nn.softmax(s, axis=-1), 0.0)


def mixer_inputs(h, w_in):
    B, T = h.shape[0], h.shape[1]
    sizes = (Q_W,) + (KV_W,) * 6 + (GATE_W, D_B, D_B, D_MODEL, D_MODEL)
    cuts = np.cumsum(sizes)[:-1].tolist()
    q, kc, vc, ks, vs, kw, vw, g, u, v, ga, gb = jnp.split(h @ w_in, cuts, axis=-1)
    kv = lambda a: a.reshape(B, T, N_KV, HEAD_DIM)
    q = q.reshape(B, T, N_KV, GROUP, HEAD_DIM) * (HEAD_DIM ** -0.5)
    gates = jax.nn.sigmoid(g.reshape(B, T, N_KV, GROUP, 3))
    return (q, gates, kv(kc), kv(vc), kv(ks), kv(vs), kv(kw), kv(vw),
            jax.nn.gelu(u), jax.nn.gelu(v), ga, gb)


def compress(k, pe, w1, w2):
    B, T = k.shape[0], k.shape[1]
    ns = T // CMP_STRIDE
    nc = ns - CMP_SUB + 1
    sub = k.reshape(B, ns, CMP_STRIDE, N_KV, HEAD_DIM).transpose(0, 1, 3, 2, 4)
    sub = sub.reshape(B, ns, N_KV, CMP_STRIDE * HEAD_DIM)
    w1p = w1.reshape(CMP_SUB, CMP_STRIDE * HEAD_DIM, CMP_HIDDEN)
    hid = sub[:, :nc] @ w1p[0]
    for i in range(1, CMP_SUB):
        hid = hid + sub[:, i:i + nc] @ w1p[i]
    hid = hid + pe.reshape(-1) @ w1
    return jax.nn.gelu(hid) @ w2


def cmp_end_positions(nc):
    return jnp.arange(nc, dtype=jnp.int32) * CMP_STRIDE + (CMP_LEN - 1)


def sel_blocks(k):
    B, T = k.shape[0], k.shape[1]
    return k.reshape(B, T // SEL_BLOCK, SEL_BLOCK, N_KV, HEAD_DIM).transpose(0, 3, 1, 2, 4)


def block_importance(imp, nsel):
    nc = imp.shape[-1]
    total = CMP_RATIO * nsel + CMP_OVER
    pad = [(0, 0)] * (imp.ndim - 1) + [(CMP_OVER, total - CMP_OVER - nc)]
    P = jnp.pad(imp, pad)
    out = P[..., 0:CMP_RATIO * nsel:CMP_RATIO]
    for o in range(1, CMP_RATIO + CMP_OVER):
        out = out + P[..., o:o + CMP_RATIO * nsel:CMP_RATIO]
    return out


def nsa_block(q, gates, qpos, kc, vc, c_end, kslb, vslb, kw, vw, wpos, slopes):
    B, Q = q.shape[0], q.shape[1]
    nsel = kslb.shape[2]
    dist_c = qpos[:, None] - c_end[None, :]
    s_c = jnp.einsum('bqgrd,bcgd->bqgrc', q, kc) - slopes[:, :, None] * dist_c[:, None, None, :]
    p_c = masked_softmax(s_c, (dist_c >= 0)[:, None, None, :])
    o_c = jnp.einsum('bqgrc,bcgd->bqgrd', p_c, vc)
    p_slc = block_importance(jnp.sum(p_c, axis=3), nsel)
    j = jnp.arange(nsel, dtype=jnp.int32)[None, :]
    cur = (qpos // SEL_BLOCK)[:, None]
    forced = (j == 0) | (j == cur) | (j == cur - 1)
    valid = (j * SEL_BLOCK) <= qpos[:, None]
    score = jnp.where(valid[None, :, None, :],
                      jnp.where(forced[None, :, None, :], FORCE_SCORE, p_slc), NEG_INF)
    _, idx = lax.top_k(score, min(SEL_TOP, nsel))
    bi = jnp.arange(B)[:, None, None, None]
    gi = jnp.arange(N_KV)[None, None, :, None]
    ks = kslb[bi, gi, idx]
    vs = vslb[bi, gi, idx]
    spos = idx[..., None] * SEL_BLOCK + jnp.arange(SEL_BLOCK, dtype=jnp.int32)
    dist_s = qpos[None, :, None, None, None] - spos
    s_s = jnp.einsum('bqgrd,bqgkjd->bqgrkj', q, ks) - slopes[None, None, :, :, None, None] * dist_s[:, :, :, None]
    mask_s = jnp.broadcast_to((dist_s >= 0)[:, :, :, None], s_s.shape)
    n_keys = idx.shape[-1] * SEL_BLOCK
    p_s = masked_softmax(s_s.reshape(B, Q, N_KV, GROUP, n_keys), mask_s.reshape(B, Q, N_KV, GROUP, n_keys))
    o_s = jnp.einsum('bqgrn,bqgnd->bqgrd', p_s, vs.reshape(B, Q, N_KV, n_keys, HEAD_DIM))
    dist_w = qpos[:, None] - wpos[None, :]
    mask_w = (dist_w >= 0) & (dist_w < WINDOW) & (wpos >= 0)[None, :]
    s_w = jnp.einsum('bqgrd,bwgd->bqgrw', q, kw) - slopes[:, :, None] * dist_w[:, None, None, :]
    p_w = masked_softmax(s_w, mask_w[:, None, None, :])
    o_w = jnp.einsum('bqgrw,bwgd->bqgrd', p_w, vw)
    o = gates[..., 0:1] * o_c + gates[..., 1:2] * o_s + gates[..., 2:3] * o_w
    return o.astype(q.dtype)


def nsa_prompt(q, gates, kcmp, vcmp, ksel, vsel, kwin, vwin, pe_k, w1_k, w2_k, pe_v, w1_v, w2_v, slopes):
    B, T = q.shape[0], q.shape[1]
    kc = compress(kcmp, pe_k, w1_k, w2_k)
    vc = compress(vcmp, pe_v, w1_v, w2_v)
    c_end = cmp_end_positions(kc.shape[1])
    kslb, vslb = sel_blocks(ksel), sel_blocks(vsel)
    pad = ((0, 0), (WINDOW, 0), (0, 0), (0, 0))
    kw_pad, vw_pad = jnp.pad(kwin, pad), jnp.pad(vwin, pad)
    nqb = T // Q_BLOCK
    qs = q.reshape(B, nqb, Q_BLOCK, N_KV, GROUP, HEAD_DIM).swapaxes(0, 1)
    gs = gates.reshape(B, nqb, Q_BLOCK, N_KV, GROUP, 3).swapaxes(0, 1)
    starts = jnp.arange(nqb, dtype=jnp.int32) * Q_BLOCK

    def one_block(args):
        qb, gb, s0 = args
        qpos = s0 + jnp.arange(Q_BLOCK, dtype=jnp.int32)
        kw = lax.dynamic_slice_in_dim(kw_pad, s0, WINDOW + Q_BLOCK, axis=1)
        vw = lax.dynamic_slice_in_dim(vw_pad, s0, WINDOW + Q_BLOCK, axis=1)
        wpos = s0 - WINDOW + jnp.arange(WINDOW + Q_BLOCK, dtype=jnp.int32)
        return nsa_block(qb, gb, qpos, kc, vc, c_end, kslb, vslb, kw, vw, wpos, slopes)

    o = lax.map(one_block, (qs, gs, starts))
    return o.swapaxes(0, 1).reshape(B, T, Q_W)


def paged_rows(pool, page_table):
    rows = pool[page_table]
    return rows.reshape(rows.shape[0], -1, N_KV, HEAD_DIM)


def nsa_sample(q, gates, kcmp, vcmp, ksel, vsel, kwin, vwin,
               cache_cmp_k, cache_cmp_v, cache_sel_k, cache_sel_v, cache_win_k, cache_win_v, page_table,
               pe_k, w1_k, w2_k, pe_v, w1_v, w2_v, slopes):
    B, n_new = q.shape[0], q.shape[1]
    past = page_table.shape[1] * PAGE_SIZE
    t_pad = -(-(past + n_new) // SEL_BLOCK) * SEL_BLOCK

    def full_rows(pool, new):
        rows = jnp.concatenate([paged_rows(pool, page_table), new], axis=1)
        return jnp.pad(rows, ((0, 0), (0, t_pad - rows.shape[1]), (0, 0), (0, 0)))

    kc = compress(full_rows(cache_cmp_k, kcmp), pe_k, w1_k, w2_k)
    vc = compress(full_rows(cache_cmp_v, vcmp), pe_v, w1_v, w2_v)
    c_end = cmp_end_positions(kc.shape[1])
    kslb = sel_blocks(full_rows(cache_sel_k, ksel))
    vslb = sel_blocks(full_rows(cache_sel_v, vsel))
    wbuf = cache_win_k.shape[1]
    kw = jnp.concatenate([cache_win_k, kwin], axis=1)
    vw = jnp.concatenate([cache_win_v, vwin], axis=1)
    wpos = past - wbuf + jnp.arange(wbuf + n_new, dtype=jnp.int32)
    qpos = past + jnp.arange(n_new, dtype=jnp.int32)
    o = nsa_block(q, gates, qpos, kc, vc, c_end, kslb, vslb, kw, vw, wpos, slopes)
    return o.reshape(B, n_new, Q_W), kw[:, -wbuf:], vw[:, -wbuf:]


def gmlp_mix(u, v, ln_g, ln_b, w_s, b_s):
    B, T = u.shape[0], u.shape[1]
    L = min(T, CHUNK)
    vn = layer_norm(v, ln_g, ln_b)
    vc = vn.reshape(B, T // L, L, N_GROUPS_B, GROUP_CH)
    w = jnp.tril(w_s[:, :L, :L])
    s = jnp.einsum('hst,bnthc->bnshc', w, vc) + b_s[:, :L].T[None, None, :, :, None]
    return u * s.reshape(B, T, D_B), vn


def merge_branches(o_a, z, ga, gb, w_branch_a, w_branch_b, w_out):
    return (jax.nn.sigmoid(ga) * (o_a @ w_branch_a) + jax.nn.sigmoid(gb) * (z @ w_branch_b)) @ w_out


def setup_inputs(seed: int = 0) -> dict:
    key = jax.random.key(seed)
    ks = jax.random.split(key, 40)
    nrm = lambda k, shape, scale: jax.random.normal(k, shape, jnp.float32) * scale
    n_pages = PAST_LEN // PAGE_SIZE
    n_used = DEC_BATCH * n_pages
    n_phys = n_used + (n_used + 3) // 4
    wbuf = min(WINDOW, PAST_LEN)
    pool = (n_phys, PAGE_SIZE, N_KV, HEAD_DIM)
    perm = jax.random.permutation(ks[9], n_phys).astype(jnp.int32)
    page_table = perm[:n_used].reshape(DEC_BATCH, n_pages)
    gain = lambda k, n: 1.0 + nrm(k, (n,), 0.02)
    return {
        'x_prompt': nrm(ks[0], (BATCH, SEQ, D_MODEL), 1.0),
        'x_sample': nrm(ks[1], (DEC_BATCH, DEC_SEQ, D_MODEL), 1.0),
        'cache_cmp_k': nrm(ks[2], pool, 1.0),
        'cache_cmp_v': nrm(ks[3], pool, 1.0),
        'cache_sel_k': nrm(ks[4], pool, 1.0),
        'cache_sel_v': nrm(ks[5], pool, 1.0),
        'cache_win_k': nrm(ks[6], (DEC_BATCH, wbuf, N_KV, HEAD_DIM), 1.0),
        'cache_win_v': nrm(ks[7], (DEC_BATCH, wbuf, N_KV, HEAD_DIM), 1.0),
        'page_table': page_table,
        'ffn1_w_in': nrm(ks[10], (D_MODEL, 2 * D_FF), D_MODEL ** -0.5),
        'ffn1_w_out': nrm(ks[11], (D_FF, D_MODEL), BETA * D_FF ** -0.5),
        'ln1_g': gain(ks[12], D_MODEL),
        'ln1_b': nrm(ks[13], (D_MODEL,), 0.02),
        'w_in': nrm(ks[14], (D_MODEL, IN_W), D_MODEL ** -0.5),
        'cmp_pe_k': nrm(ks[15], (CMP_LEN, HEAD_DIM), 0.02),
        'cmp_w1_k': nrm(ks[16], (CMP_LEN * HEAD_DIM, CMP_HIDDEN), (CMP_LEN * HEAD_DIM) ** -0.5),
        'cmp_w2_k': nrm(ks[17], (CMP_HIDDEN, HEAD_DIM), CMP_HIDDEN ** -0.5),
        'cmp_pe_v': nrm(ks[18], (CMP_LEN, HEAD_DIM), 0.02),
        'cmp_w1_v': nrm(ks[19], (CMP_LEN * HEAD_DIM, CMP_HIDDEN), (CMP_LEN * HEAD_DIM) ** -0.5),
        'cmp_w2_v': nrm(ks[20], (CMP_HIDDEN, HEAD_DIM), CMP_HIDDEN ** -0.5),
        'gmlp_ln_g': gain(ks[21], D_B),
        'gmlp_ln_b': nrm(ks[22], (D_B,), 0.02),
        'spatial_w': nrm(ks[23], (N_GROUPS_B, CHUNK, CHUNK), CHUNK ** -0.5),
        'spatial_b': 1.0 + nrm(ks[24], (N_GROUPS_B, CHUNK), 0.02),
        'w_branch_a': nrm(ks[25], (Q_W, D_MODEL), Q_W ** -0.5),
        'w_branch_b': nrm(ks[26], (D_B, D_MODEL), D_B ** -0.5),
        'w_out': nrm(ks[27], (D_MODEL, D_MODEL), BETA * D_MODEL ** -0.5),
        'ln2_g': gain(ks[28], D_MODEL),
        'ln2_b': nrm(ks[29], (D_MODEL,), 0.02),
        'ffn2_w_in': nrm(ks[30], (D_MODEL, 2 * D_FF), D_MODEL ** -0.5),
        'ffn2_w_out': nrm(ks[31], (D_FF, D_MODEL), BETA * D_FF ** -0.5),
        'ln3_g': gain(ks[32], D_MODEL),
        'ln3_b': nrm(ks[33], (D_MODEL,), 0.02),
    }


def reference(x_prompt, x_sample, cache_cmp_k, cache_cmp_v, cache_sel_k, cache_sel_v, cache_win_k, cache_win_v,
              page_table, ffn1_w_in, ffn1_w_out, ln1_g, ln1_b, w_in, cmp_pe_k, cmp_w1_k, cmp_w2_k,
              cmp_pe_v, cmp_w1_v, cmp_w2_v, gmlp_ln_g, gmlp_ln_b, spatial_w, spatial_b,
              w_branch_a, w_branch_b, w_out, ln2_g, ln2_b, ffn2_w_in, ffn2_w_out, ln3_g, ln3_b):
    slopes = alibi_slopes()
    for _ in range(DEPTH):
        h = half_ffn_block(x_prompt, ffn1_w_in, ffn1_w_out, ln1_g, ln1_b)
        (q, gates, kc, vc, ksl, vsl, kw, vw, u, v, ga, gb) = mixer_inputs(h, w_in)
        o_a = nsa_prompt(q, gates, kc, vc, ksl, vsl, kw, vw,
                         cmp_pe_k, cmp_w1_k, cmp_w2_k, cmp_pe_v, cmp_w1_v, cmp_w2_v, slopes)
        z, _ = gmlp_mix(u, v, gmlp_ln_g, gmlp_ln_b, spatial_w, spatial_b)
        h = layer_norm(ALPHA * h + merge_branches(o_a, z, ga, gb, w_branch_a, w_branch_b, w_out), ln2_g, ln2_b)
        y_prompt = half_ffn_block(h, ffn2_w_in, ffn2_w_out, ln3_g, ln3_b)
        wkeep = min(WINDOW, kw.shape[1])
        p_cmp_k, p_cmp_v, p_sel_k, p_sel_v = kc, vc, ksl, vsl
        p_win_k, p_win_v = kw[:, -wkeep:], vw[:, -wkeep:]
        hs = half_ffn_block(x_sample, ffn1_w_in, ffn1_w_out, ln1_g, ln1_b)
        (qs, gates_s, kcs, vcs, ksls, vsls, kws, vws, us, vs_, gas, gbs) = mixer_inputs(hs, w_in)
        o_as, s_win_k, s_win_v = nsa_sample(qs, gates_s, kcs, vcs, ksls, vsls, kws, vws,
                                             cache_cmp_k, cache_cmp_v, cache_sel_k, cache_sel_v,
                                             cache_win_k, cache_win_v, page_table,
                                             cmp_pe_k, cmp_w1_k, cmp_w2_k, cmp_pe_v, cmp_w1_v, cmp_w2_v, slopes)
        zs, s_chunk_v = gmlp_mix(us, vs_, gmlp_ln_g, gmlp_ln_b, spatial_w, spatial_b)
        hs = layer_norm(ALPHA * hs + merge_branches(o_as, zs, gas, gbs, w_branch_a, w_branch_b, w_out), ln2_g, ln2_b)
        y_sample = half_ffn_block(hs, ffn2_w_in, ffn2_w_out, ln3_g, ln3_b)
    return (y_prompt, y_sample, p_cmp_k, p_cmp_v, p_sel_k, p_sel_v, p_win_k, p_win_v,
            kcs, vcs, ksls, vsls, s_win_k, s_win_v, s_chunk_v)
```

```python
import functools

import numpy as np
import jax
import jax.numpy as jnp
from jax import lax
from jax.experimental import pallas as pl
from jax.experimental.pallas import tpu as pltpu

F32 = jnp.float32
BF16 = jnp.bfloat16

D_MODEL = 1024
N_HEADS = 8
HEAD_DIM = 64
N_KV = 2
GROUP = N_HEADS // N_KV
PAGE_SIZE = 128
CMP_LEN = 32
CMP_STRIDE = 16
CMP_HIDDEN = 128
SEL_BLOCK = 64
CMP_RATIO = SEL_BLOCK // CMP_STRIDE
SEL_TOP = 16
WINDOW = 512
Q_BLOCK = 128
FORCE_SCORE = 1e4
D_B = 512
N_GROUPS_B = 4
GROUP_CH = D_B // N_GROUPS_B
CHUNK = 128
D_FF = 2816
DEPTH = 1
ALPHA = (2.0 * DEPTH) ** 0.25
LN_EPS = 1e-5
NEG_INF = -1e30
REMOVED = -3e38
Q_W = N_HEADS * HEAD_DIM
KV_W = N_KV * HEAD_DIM
GATE_W = N_HEADS * 3
SLAB = N_KV * HEAD_DIM
Q_EXP = N_HEADS * SLAB
SUB_W = CMP_STRIDE * SLAB
VMEM_LIMIT = 56 * 1024 * 1024
SEL_TILE = 512


def _params(sem, vmem=VMEM_LIMIT):
    return pltpu.CompilerParams(dimension_semantics=sem, vmem_limit_bytes=vmem)


def _layer_norm(x, g, b):
    mu = jnp.mean(x, axis=-1, keepdims=True)
    xc = x - mu
    var = jnp.mean(xc * xc, axis=-1, keepdims=True)
    return xc * lax.rsqrt(var + LN_EPS) * g + b


def _dot(a, b):
    return jnp.dot(a, b, preferred_element_type=F32)


def _dot_nt(a, b):
    return lax.dot_general(a, b, (((1,), (1,)), ((), ())), preferred_element_type=F32)


def _split3(x):
    hi = x.astype(BF16)
    r1 = x - hi.astype(F32)
    mid = r1.astype(BF16)
    lo = (r1 - mid.astype(F32)).astype(BF16)
    return hi, mid, lo


def _dot_exact_rhs(x, a_bf16):
    hi, mid, lo = _split3(x)
    return _dot(hi, a_bf16) + _dot(mid, a_bf16) + _dot(lo, a_bf16)


def _ffn_ln_kernel(x_ref, wg_ref, wu_ref, wo_ref, g_ref, b_ref, o_ref, xb_ref, acc_ref, *, n_f):
    f = pl.program_id(1)

    @pl.when(f == 0)
    def _():
        xb_ref[...] = x_ref[...].astype(BF16)
        acc_ref[...] = jnp.zeros_like(acc_ref)

    xb = xb_ref[...]
    gate = _dot(xb, wg_ref[...])
    up = _dot(xb, wu_ref[...])
    hid = (gate * jax.nn.sigmoid(gate)) * up
    acc_ref[...] += _dot(hid.astype(BF16), wo_ref[...])

    @pl.when(f == n_f - 1)
    def _():
        y = ALPHA * x_ref[...] + 0.5 * acc_ref[...]
        o_ref[...] = _layer_norm(y, g_ref[...], b_ref[...])


def ffn_ln(x, w_in_b, w_out_b, g, b, *, tm, tf=256):
    m, d = x.shape
    d_ff = w_out_b.shape[0]
    n_f = d_ff // tf
    return pl.pallas_call(
        functools.partial(_ffn_ln_kernel, n_f=n_f),
        grid=(m // tm, n_f),
        in_specs=[
            pl.BlockSpec((tm, d), lambda i, f: (i, 0)),
            pl.BlockSpec((d, tf), lambda i, f: (0, f)),
            pl.BlockSpec((d, tf), lambda i, f: (0, f + n_f)),
            pl.BlockSpec((tf, d), lambda i, f: (f, 0)),
            pl.BlockSpec((1, d), lambda i, f: (0, 0)),
            pl.BlockSpec((1, d), lambda i, f: (0, 0)),
        ],
        out_specs=pl.BlockSpec((tm, d), lambda i, f: (i, 0)),
        out_shape=jax.ShapeDtypeStruct((m, d), F32),
        scratch_shapes=[pltpu.VMEM((tm, d), BF16), pltpu.VMEM((tm, d), F32)],
        compiler_params=_params(("parallel", "arbitrary")),
        name="ffn_ln",
    )(x, w_in_b, w_in_b, w_out_b, g.reshape(1, d), b.reshape(1, d))


def _mixer_proj_kernel(h_ref, wq_ref, wkv_ref, wgt_ref, wuv_ref, wgab_ref,
                       q_ref, kc_ref, vc_ref, ks_ref, vs_ref, kw_ref, vw_ref,
                       kvb_ref, gt_ref, uv_ref, sgab_ref):
    hb = h_ref[...].astype(BF16)
    q_ref[...] = (_dot(hb, wq_ref[...]) * (HEAD_DIM ** -0.5)).astype(BF16)
    kv = _dot(hb, wkv_ref[...])
    for j, ref in enumerate((kc_ref, vc_ref, ks_ref, vs_ref, kw_ref, vw_ref)):
        ref[...] = kv[:, j * SLAB:(j + 1) * SLAB]
    kvb_ref[...] = kv.astype(BF16)
    gt_ref[...] = jax.nn.sigmoid(_dot(hb, wgt_ref[...]))
    uv_ref[...] = jax.nn.gelu(_dot(hb, wuv_ref[...]))
    sgab_ref[...] = jax.nn.sigmoid(_dot(hb, wgab_ref[...])).astype(BF16)


def mixer_proj(h, wq, wkv, wgt, wuv, wgab, *, tm):
    m, d = h.shape
    row = lambda n: pl.BlockSpec((tm, n), lambda i: (i, 0))
    full = lambda w: pl.BlockSpec(w.shape, lambda i: (0, 0))
    out_shape = (
        [jax.ShapeDtypeStruct((m, Q_EXP), BF16)]
        + [jax.ShapeDtypeStruct((m, SLAB), F32)] * 6
        + [jax.ShapeDtypeStruct((m, 6 * SLAB), BF16),
           jax.ShapeDtypeStruct((m, 128), F32),
           jax.ShapeDtypeStruct((m, 2 * D_B), F32),
           jax.ShapeDtypeStruct((m, 2 * D_MODEL), BF16)])
    out_specs = ([row(Q_EXP)] + [row(SLAB)] * 6
                 + [row(6 * SLAB), row(128), row(2 * D_B), row(2 * D_MODEL)])
    return pl.pallas_call(
        _mixer_proj_kernel,
        grid=(m // tm,),
        in_specs=[row(d), full(wq), full(wkv), full(wgt), full(wuv), full(wgab)],
        out_specs=out_specs,
        out_shape=out_shape,
        compiler_params=_params(("parallel",)),
        name="mixer_proj",
    )(h, wq, wkv, wgt, wuv, wgab)


def _cmp_ab_kernel(pt_ref, *refs, n_pages):
    del pt_ref
    page_refs = refs[:n_pages]
    w_ref = refs[n_pages]
    o_ref = refs[n_pages + 1]
    if n_pages == 1:
        x = page_refs[0][...]
    else:
        x = jnp.concatenate([r[...] for r in page_refs], axis=0)
    o_ref[...] = _dot(x.astype(BF16), w_ref[...])


def cmp_ab(pool, page_table, w_ab, *, pages_per_step):
    nb, n_pages = page_table.shape
    rows = pool.shape[1]
    pp = pages_per_step
    n_steps = n_pages // pp

    def page_spec(k):
        return pl.BlockSpec((None, rows, SUB_W), lambda b, s, pt: (pt[b, s * pp + k], 0, 0))

    grid_spec = pltpu.PrefetchScalarGridSpec(
        num_scalar_prefetch=1,
        grid=(nb, n_steps),
        in_specs=[page_spec(k) for k in range(pp)]
        + [pl.BlockSpec(w_ab.shape, lambda b, s, pt: (0, 0))],
        out_specs=pl.BlockSpec((None, pp * rows, w_ab.shape[1]), lambda b, s, pt: (b, s, 0)),
    )
    return pl.pallas_call(
        functools.partial(_cmp_ab_kernel, n_pages=pp),
        grid_spec=grid_spec,
        out_shape=jax.ShapeDtypeStruct((nb, n_pages * rows, w_ab.shape[1]), F32),
        compiler_params=_params(("parallel", "arbitrary")),
        name="cmp_ab",
    )(page_table, *([pool] * pp), w_ab)


def _cmp_fin_kernel(*refs, has_tail):
    if has_tail:
        ab_ref, tail_ref, pe_ref, w1_ref, w2_ref, o_ref = refs
    else:
        ab_ref, pe_ref, w1_ref, w2_ref, o_ref = refs
    hw = 2 * CMP_HIDDEN
    n = ab_ref.shape[0]
    pe_h = _dot(pe_ref[...], w1_ref[...])[0:1, :]
    bias = jnp.concatenate([pe_h, pe_h], axis=1)
    w2 = w2_ref[...]
    first = ab_ref[:, 0:hw]
    second = pltpu.roll(ab_ref[:, hw:2 * hw], n - 1, 0)
    if has_tail:
        t_first = tail_ref[:, 0:hw]
        t_second = tail_ref[:, hw:2 * hw]
        nt = tail_ref.shape[0]
        is_last = lax.broadcasted_iota(jnp.int32, (n, 1), 0) == n - 1
        second = jnp.where(is_last, t_second[0:1, :], second)
        t_hid = t_first + pltpu.roll(t_second, nt - 1, 0) + bias
        o_ref[n:n + nt, :] = _dot(jax.nn.gelu(t_hid).astype(BF16), w2).astype(BF16)
        n_out = o_ref.shape[0]
        if n_out > n + nt:
            o_ref[n + nt:n_out, :] = jnp.zeros((n_out - n - nt, SLAB), BF16)
    hid = first + second + bias
    o_ref[0:n, :] = _dot(jax.nn.gelu(hid).astype(BF16), w2).astype(BF16)


def cmp_fin(ab, tail, pe8, w1b, w2x):
    nb, n, w = ab.shape
    has_tail = tail is not None
    nt = tail.shape[1] if has_tail else 0
    full = lambda a: pl.BlockSpec(a.shape, lambda b: (0, 0))
    in_specs = [pl.BlockSpec((None, n, w), lambda b: (b, 0, 0))]
    args = [ab]
    if has_tail:
        in_specs.append(pl.BlockSpec((None, nt, w), lambda b: (b, 0, 0)))
        args.append(tail)
    in_specs += [full(pe8), full(w1b), full(w2x)]
    args += [pe8, w1b, w2x]
    n_out = -(-(n + nt) // 128) * 128 if has_tail else n
    return pl.pallas_call(
        functools.partial(_cmp_fin_kernel, has_tail=has_tail),
        grid=(nb,),
        in_specs=in_specs,
        out_specs=pl.BlockSpec((None, n_out, SLAB), lambda b: (b, 0, 0)),
        out_shape=jax.ShapeDtypeStruct((nb, n_out, SLAB), BF16),
        compiler_params=_params(("parallel",)),
        name="cmp_fin",
    )(*args)


def _slope_of_head(h):
    return 2.0 ** (-(h + 1))


def _slope_column():
    head = lax.broadcasted_iota(jnp.int32, (N_HEADS, 1), 0)
    slope = jnp.zeros((N_HEADS, 1), F32)
    for h in range(N_HEADS):
        slope = jnp.where(head == h, _slope_of_head(h), slope)
    return slope


def _top_k_mask_t(score_t, k):
    n = score_t.shape[0]
    cand_idx = lax.broadcasted_iota(jnp.int32, score_t.shape, 0).astype(F32)

    def body(_, carry):
        s, sel = carry
        m = jnp.max(s, axis=0, keepdims=True)
        first = jnp.min(jnp.where(s == m, cand_idx, float(n)), axis=0, keepdims=True)
        pick = cand_idx == first
        return jnp.where(pick, REMOVED, s), jnp.where(pick, 1.0, sel)

    _, sel = lax.fori_loop(0, k, body, (score_t, jnp.zeros_like(score_t)), unroll=True)
    return sel


def _softmax_masked(s, mask):
    s = jnp.where(mask, s, NEG_INF)
    m = jnp.max(s, axis=-1, keepdims=True)
    e = jnp.exp(s - m)
    return jnp.where(mask, e / jnp.sum(e, axis=-1, keepdims=True), 0.0)


def _nsa_prompt_kernel(q_ref, gt_ref, kc_ref, vc_ref, ks_ref, vs_ref, kw_ref, vw_ref, imp_ref,
                       o_ref):
    qb = pl.program_id(1)
    s0 = qb * Q_BLOCK
    nq = Q_BLOCK
    nc = kc_ref.shape[0]
    n_sel = ks_ref.shape[0] // SEL_BLOCK
    gates = gt_ref[...]

    q_loc = lax.broadcasted_iota(jnp.int32, (nq, 1), 0)
    qpos = s0 + q_loc
    assert HEAD_DIM == 64 and SEL_BLOCK == 64
    lane_half = lax.broadcasted_iota(jnp.int32, (1, SLAB), 1) >> 6

    c_end = lax.broadcasted_iota(jnp.int32, (1, nc), 1) * CMP_STRIDE + (CMP_LEN - 1)
    dist_c = qpos - c_end
    mask_c = dist_c >= 0
    dist_c_f = dist_c.astype(F32)

    j_idx = lax.broadcasted_iota(jnp.int32, (1, n_sel), 1)
    cur = qpos >> 6
    forced = (j_idx == 0) | (j_idx == cur) | (j_idx == cur - 1)
    valid = (j_idx * SEL_BLOCK) <= qpos

    w0 = pl.multiple_of(jnp.maximum(s0 - WINDOW, 0), Q_BLOCK)
    n_win = WINDOW + nq
    wpos = w0 + lax.broadcasted_iota(jnp.int32, (1, n_win), 1)
    dist_w = qpos - wpos
    mask_w = (dist_w >= 0) & (dist_w < WINDOW)
    dist_w_f = dist_w.astype(F32)
    kw = kw_ref[pl.ds(w0, n_win), :]
    vw = vw_ref[pl.ds(w0, n_win), :]

    n_tiles = (s0 + nq + SEL_TILE - 1) // SEL_TILE
    k_loc = lax.broadcasted_iota(jnp.int32, (1, SEL_TILE), 1)
    rel = q_loc - k_loc
    blk_of_key = k_loc >> 6
    j_col = lax.broadcasted_iota(jnp.int32, (n_sel, 1), 0)

    for g in range(N_KV):
        heads = [g * GROUP + r for r in range(GROUP)]
        q_heads = [q_ref[:, h * SLAB:(h + 1) * SLAB] for h in heads]

        p_sum = jnp.zeros((nq, nc), F32)
        o_cmp = []
        for r, h in enumerate(heads):
            s = _dot_nt(q_heads[r], kc_ref[...]) - _slope_of_head(h) * dist_c_f
            p = _softmax_masked(s, mask_c)
            p_sum = p_sum + p
            o_cmp.append(_dot(p.astype(BF16), vc_ref[...]))

        p_slc = _dot_exact_rhs(p_sum, imp_ref[...])
        score = jnp.where(valid, jnp.where(forced, FORCE_SCORE, p_slc), NEG_INF)
        sel = _top_k_mask_t(score.T, min(SEL_TOP, n_sel)).T.astype(BF16)

        def sel_tile(t, carry):
            kt = pl.multiple_of(t * SEL_TILE, SEL_TILE)
            k = ks_ref[pl.ds(kt, SEL_TILE), :]
            v = vs_ref[pl.ds(kt, SEL_TILE), :]
            expand = jnp.where((j_col - t * (SEL_TILE // SEL_BLOCK)) == blk_of_key,
                               1.0, 0.0).astype(BF16)
            dist = rel + (s0 - kt)
            mask = (_dot(sel, expand) > 0.5) & (dist >= 0)
            dist_f = dist.astype(F32)
            out = []
            for r, h in enumerate(heads):
                m_old, l_old, acc_old = carry[r]
                s = _dot_nt(q_heads[r], k) - _slope_of_head(h) * dist_f
                s = jnp.where(mask, s, NEG_INF)
                m_new = jnp.maximum(m_old, jnp.max(s, axis=-1, keepdims=True))
                alpha = jnp.exp(m_old - m_new)
                p = jnp.exp(s - m_new)
                l_new = alpha * l_old + jnp.sum(p, axis=-1, keepdims=True)
                acc_new = alpha * acc_old + _dot(p.astype(BF16), v)
                out.append((m_new, l_new, acc_new))
            return tuple(out)

        init = tuple((jnp.full((nq, 1), NEG_INF, F32), jnp.zeros((nq, 1), F32),
                      jnp.zeros((nq, SLAB), F32)) for _ in heads)
        sel_state = lax.fori_loop(0, n_tiles, sel_tile, init)

        for r, h in enumerate(heads):
            s = _dot_nt(q_heads[r], kw) - _slope_of_head(h) * dist_w_f
            o_win = _dot(_softmax_masked(s, mask_w).astype(BF16), vw)
            _, l_sel, acc_sel = sel_state[r]
            o_sel = acc_sel / l_sel
            o = (gates[:, 3 * h:3 * h + 1] * o_cmp[r] + gates[:, 3 * h + 1:3 * h + 2] * o_sel
                 + gates[:, 3 * h + 2:3 * h + 3] * o_win)
            o_ref[:, h * SLAB:(h + 1) * SLAB] = jnp.where(lane_half == g, o, 0.0).astype(BF16)


def nsa_prompt(q, gates, kc, vc, kvb, imp):
    nb, t, _ = q.shape
    nc = kc.shape[1]
    seq = lambda j: pl.BlockSpec((None, t, SLAB), lambda b, i: (b, 0, j))
    return pl.pallas_call(
        _nsa_prompt_kernel,
        grid=(nb, t // Q_BLOCK),
        in_specs=[
            pl.BlockSpec((None, Q_BLOCK, Q_EXP), lambda b, i: (b, i, 0)),
            pl.BlockSpec((None, Q_BLOCK, 128), lambda b, i: (b, i, 0)),
            pl.BlockSpec((None, nc, SLAB), lambda b, i: (b, 0, 0)),
            pl.BlockSpec((None, nc, SLAB), lambda b, i: (b, 0, 0)),
            seq(2), seq(3), seq(4), seq(5),
            pl.BlockSpec(imp.shape, lambda b, i: (0, 0)),
        ],
        out_specs=pl.BlockSpec((None, Q_BLOCK, Q_EXP), lambda b, i: (b, i, 0)),
        out_shape=jax.ShapeDtypeStruct((nb, t, Q_EXP), BF16),
        compiler_params=_params(("parallel", "arbitrary")),
        name="nsa_prompt",
    )(q, gates, kc, vc, kvb, kvb, kvb, kvb, imp)


def _nsa_sample_cmp_kernel(q_ref, kc_ref, vc_ref, imp_ref, oc_ref, idx_ref, *, qpos, n_sel):
    nc = kc_ref.shape[0]
    n_cand = imp_ref.shape[1]
    q = q_ref[...]
    head = lax.broadcasted_iota(jnp.int32, (N_HEADS, 1), 0)
    slope = _slope_column()
    c_end = lax.broadcasted_iota(jnp.int32, (1, nc), 1) * CMP_STRIDE + (CMP_LEN - 1)
    dist = qpos - c_end
    mask = dist >= 0
    s = _dot_nt(q, kc_ref[...]) - slope * dist.astype(F32)
    p = _softmax_masked(s, mask)
    oc_ref[...] = _dot(p.astype(BF16), vc_ref[...])

    p_sum = jnp.zeros_like(p)
    for g in range(N_KV):
        grp = jnp.sum(p[g * GROUP:(g + 1) * GROUP], axis=0, keepdims=True)
        p_sum = jnp.where(head == g, grp, p_sum)
    p_slc = _dot_exact_rhs(p_sum, imp_ref[...])

    j_idx = lax.broadcasted_iota(jnp.int32, (1, n_cand), 1)
    cur = qpos // SEL_BLOCK
    forced = (j_idx == 0) | (j_idx == cur) | (j_idx == cur - 1)
    valid = (j_idx * SEL_BLOCK) <= qpos
    score = jnp.where(valid, jnp.where(forced, FORCE_SCORE, p_slc), NEG_INF)
    score = jnp.where(j_idx < n_sel, score, REMOVED)

    cand = lax.broadcasted_iota(jnp.int32, score.shape, 1).astype(F32)
    out_lane = lax.broadcasted_iota(jnp.int32, (N_HEADS, 128), 1)

    def body(k, carry):
        sc, out = carry
        m = jnp.max(sc, axis=1, keepdims=True)
        first = jnp.min(jnp.where(sc == m, cand, float(n_cand)), axis=1, keepdims=True)
        return (jnp.where(cand == first, REMOVED, sc),
                jnp.where(out_lane == k, first.astype(jnp.int32), out))

    _, out = lax.fori_loop(0, SEL_TOP, body, (score, jnp.zeros((N_HEADS, 128), jnp.int32)),
                           unroll=True)
    idx_ref[...] = out


def nsa_sample_cmp(q8, kc, vc, imp, *, qpos, n_sel):
    nb = q8.shape[0]
    nc = kc.shape[1]
    return pl.pallas_call(
        functools.partial(_nsa_sample_cmp_kernel, qpos=qpos, n_sel=n_sel),
        grid=(nb,),
        in_specs=[
            pl.BlockSpec((None, N_HEADS, SLAB), lambda b: (b, 0, 0)),
            pl.BlockSpec((None, nc, SLAB), lambda b: (b, 0, 0)),
            pl.BlockSpec((None, nc, SLAB), lambda b: (b, 0, 0)),
            pl.BlockSpec(imp.shape, lambda b: (0, 0)),
        ],
        out_specs=[pl.BlockSpec((None, N_HEADS, SLAB), lambda b: (b, 0, 0)),
                   pl.BlockSpec((None, N_HEADS, 128), lambda b: (b, 0, 0))],
        out_shape=[jax.ShapeDtypeStruct((nb, N_HEADS, SLAB), F32),
                   jax.ShapeDtypeStruct((nb, N_HEADS, 128), jnp.int32)],
        compiler_params=_params(("parallel",)),
        name="nsa_sample_cmp",
    )(q8, kc, vc, imp)


def _nsa_sample_sel_kernel(pt_ref, idx_ref, *refs, qpos, n_past_blocks):
    del pt_ref
    n_pg = N_KV * SEL_TOP
    k_pages = refs[:n_pg]
    v_pages = refs[n_pg:2 * n_pg]
    (q_ref, gt_ref, oc_ref, kn_ref, vn_ref, wk_ref, wv_ref, wkn_ref, wvn_ref,
     o_ref) = refs[2 * n_pg:]
    b = pl.program_id(0)
    q = q_ref[...]
    head = lax.broadcasted_iota(jnp.int32, (N_HEADS, 1), 0)
    slope = _slope_column()
    grp_of_row = head >> 2
    assert GROUP == 4 and HEAD_DIM == 64 and SEL_BLOCK == 64
    lane_half = lax.broadcasted_iota(jnp.int32, (1, SLAB), 1) >> 6
    blocks_per_page = PAGE_SIZE // SEL_BLOCK
    n_keys = SEL_TOP * SEL_BLOCK
    key_slot = lax.broadcasted_iota(jnp.int32, (1, n_keys), 1) >> 6
    key_off = lax.broadcasted_iota(jnp.int32, (1, n_keys), 1) & (SEL_BLOCK - 1)
    qf = q.astype(F32)

    s_new = jnp.sum(qf * kn_ref[...], axis=1, keepdims=True)

    o_sel = jnp.zeros((N_HEADS, SLAB), F32)
    for g in range(N_KV):
        ks, vs = [], []
        start = jnp.zeros((1, n_keys), jnp.int32)
        for k in range(SEL_TOP):
            j = idx_ref[b, g * SEL_TOP + k]
            off = pl.multiple_of((j % blocks_per_page) * SEL_BLOCK, SEL_BLOCK)
            ks.append(k_pages[g * SEL_TOP + k][pl.ds(off, SEL_BLOCK), :])
            vs.append(v_pages[g * SEL_TOP + k][pl.ds(off, SEL_BLOCK), :])
            start = jnp.where(key_slot == k, j * SEL_BLOCK, start)
        k_all = jnp.concatenate(ks, axis=0).astype(BF16)
        v_all = jnp.concatenate(vs, axis=0).astype(BF16)
        spos = start + key_off
        dist = qpos - spos
        mask = (dist >= 0) & (start < n_past_blocks * SEL_BLOCK)
        s = _dot_nt(q, k_all) - slope * dist.astype(F32)
        s = jnp.where(mask, s, NEG_INF)
        m = jnp.maximum(jnp.max(s, axis=-1, keepdims=True), s_new)
        e = jnp.exp(s - m)
        e_new = jnp.exp(s_new - m)
        l = jnp.sum(e, axis=-1, keepdims=True) + e_new
        o_g = (_dot(e.astype(BF16), v_all) + e_new * vn_ref[...]) / l
        o_sel = jnp.where(grp_of_row == g, o_g, o_sel)

    n_win = wk_ref.shape[0]
    dist_w = n_win - lax.broadcasted_iota(jnp.int32, (1, n_win), 1)
    mask_w = (dist_w >= 0) & (dist_w < WINDOW)
    s_w = _dot_nt(q, wk_ref[...].astype(BF16)) - slope * dist_w.astype(F32)
    s_w = jnp.where(mask_w, s_w, NEG_INF)
    s_wn = jnp.sum(qf * wkn_ref[...], axis=1, keepdims=True)
    m_w = jnp.maximum(jnp.max(s_w, axis=-1, keepdims=True), s_wn)
    e_w = jnp.exp(s_w - m_w)
    e_wn = jnp.exp(s_wn - m_w)
    l_w = jnp.sum(e_w, axis=-1, keepdims=True) + e_wn
    o_win = (_dot(e_w.astype(BF16), wv_ref[...].astype(BF16)) + e_wn * wvn_ref[...]) / l_w

    gt = gt_ref[...]
    o = gt[:, 0:1] * oc_ref[...] + gt[:, 1:2] * o_sel + gt[:, 2:3] * o_win
    o_ref[...] = jnp.where(lane_half == grp_of_row, o, 0.0).astype(BF16)


def nsa_sample_sel(page_table, idx, pool_k, pool_v, q8, gt8, o_cmp, k_new, v_new,
                   win_k, win_v, wk_new, wv_new, *, qpos):
    nb, n_pages = page_table.shape
    n_win = win_k.shape[1]
    blocks_per_page = PAGE_SIZE // SEL_BLOCK

    def page_spec(i):
        def index_map(b, pt, ix):
            page = jnp.minimum(ix[b, i] // blocks_per_page, n_pages - 1)
            return (pt[b, page], 0, 0)
        return pl.BlockSpec((None, PAGE_SIZE, SLAB), index_map)

    per_b = lambda r: pl.BlockSpec((None, r, SLAB), lambda b, pt, ix: (b, 0, 0))
    n_pg = N_KV * SEL_TOP
    grid_spec = pltpu.PrefetchScalarGridSpec(
        num_scalar_prefetch=2,
        grid=(nb,),
        in_specs=[page_spec(i) for i in range(n_pg)] * 2
        + [per_b(N_HEADS), per_b(N_HEADS), per_b(N_HEADS), per_b(1), per_b(1),
           per_b(n_win), per_b(n_win), per_b(1), per_b(1)],
        out_specs=per_b(N_HEADS),
    )
    return pl.pallas_call(
        functools.partial(_nsa_sample_sel_kernel, qpos=qpos, n_past_blocks=n_pages * blocks_per_page),
        grid_spec=grid_spec,
        out_shape=jax.ShapeDtypeStruct((nb, N_HEADS, SLAB), BF16),
        compiler_params=_params(("arbitrary",)),
        name="nsa_sample_sel",
    )(page_table, idx, *([pool_k] * n_pg), *([pool_v] * n_pg), q8, gt8, o_cmp, k_new, v_new,
      win_k, win_v, wk_new, wv_new)


def _merge_ln_kernel(*refs, chunked):
    if chunked:
        (h_ref, oa_ref, uv_ref, sg_ref, wa_ref, wb_ref, wo_ref, g2_ref, b2_ref, lg_ref, lb_ref,
         sw_ref, sb_ref, o_ref) = refs
    else:
        (h_ref, oa_ref, uv_ref, sg_ref, wa_ref, wb_ref, wo_ref, g2_ref, b2_ref, lg_ref, lb_ref,
         sw_ref, sb_ref, o_ref, vn_ref) = refs
    tm = h_ref.shape[0]
    u = uv_ref[:, 0:D_B]
    vn = _layer_norm(uv_ref[:, D_B:2 * D_B], lg_ref[...], lb_ref[...])
    if chunked:
        row = lax.broadcasted_iota(jnp.int32, (CHUNK, CHUNK), 0)
        col = lax.broadcasted_iota(jnp.int32, (CHUNK, CHUNK), 1)
        vnb = vn.astype(BF16)
        chunks = []
        for c in range(tm // CHUNK):
            parts = []
            for hg in range(N_GROUPS_B):
                w = jnp.where(col <= row, sw_ref[hg], 0.0).astype(BF16)
                parts.append(_dot(w, vnb[c * CHUNK:(c + 1) * CHUNK, hg * GROUP_CH:(hg + 1) * GROUP_CH]))
            chunks.append(jnp.concatenate(parts, axis=1) + sb_ref[...])
        s = jnp.concatenate(chunks, axis=0)
    else:
        s = vn * sw_ref[...] + sb_ref[...]
        vn_ref[...] = vn
    z = u * s
    branch_a = _dot(oa_ref[...], wa_ref[...])
    branch_b = _dot(z.astype(BF16), wb_ref[...])
    merged = (sg_ref[:, 0:D_MODEL].astype(F32) * branch_a
              + sg_ref[:, D_MODEL:2 * D_MODEL].astype(F32) * branch_b)
    y = ALPHA * h_ref[...] + _dot(merged.astype(BF16), wo_ref[...])
    o_ref[...] = _layer_norm(y, g2_ref[...], b2_ref[...])


def merge_ln(h, oa, uv, sgab, wa, wb, wo, g2, b2, lg, lb, sw, sb, *, tm, chunked):
    m, d = h.shape
    row = lambda n: pl.BlockSpec((tm, n), lambda i: (i, 0))
    full = lambda a: pl.BlockSpec(a.shape, lambda i: (0,) * a.ndim)
    vec = lambda a: a.reshape(1, -1)
    g2, b2, lg, lb = vec(g2), vec(b2), vec(lg), vec(lb)
    out_shape = [jax.ShapeDtypeStruct((m, d), F32)]
    out_specs = [row(d)]
    if not chunked:
        out_shape.append(jax.ShapeDtypeStruct((m, D_B), F32))
        out_specs.append(row(D_B))
    res = pl.pallas_call(
        functools.partial(_merge_ln_kernel, chunked=chunked),
        grid=(m // tm,),
        in_specs=[row(d), row(Q_EXP), row(2 * D_B), row(2 * D_MODEL), full(wa), full(wb), full(wo),
                  full(g2), full(b2), full(lg), full(lb), full(sw), full(sb)],
        out_specs=out_specs,
        out_shape=out_shape,
        compiler_params=_params(("parallel",)),
        name="merge_ln",
    )(h, oa, uv, sgab, wa, wb, wo, g2, b2, lg, lb, sw, sb)
    return res if not chunked else res[0]


def _group_mask():
    return (np.arange(N_HEADS)[:, None] // GROUP == np.arange(N_KV)[None, :]).astype(np.float32)


def _split_w_in(w_in):
    sizes = (Q_W,) + (KV_W,) * 6 + (GATE_W, D_B, D_B, D_MODEL, D_MODEL)
    cuts = np.cumsum(sizes)[:-1].tolist()
    q, kc, vc, ks, vs, kw, vw, g, u, v, ga, gb = jnp.split(w_in, cuts, axis=1)
    d = w_in.shape[0]
    gm = jnp.asarray(_group_mask())
    wq = (q.reshape(d, N_HEADS, 1, HEAD_DIM) * gm[None, :, :, None]).reshape(d, Q_EXP)
    wkv = jnp.concatenate([kc, vc, ks, vs, kw, vw], axis=1)
    wgt = jnp.pad(g, ((0, 0), (0, 128 - GATE_W)))
    wuv = jnp.concatenate([u, v], axis=1)
    wgab = jnp.concatenate([ga, gb], axis=1)
    return tuple(w.astype(BF16) for w in (wq, wkv, wgt, wuv, wgab))


def _expand_branch_a(w_a):
    gm = jnp.asarray(_group_mask())
    w = w_a.reshape(N_HEADS, 1, HEAD_DIM, w_a.shape[1]) * gm[:, :, None, None]
    return w.reshape(Q_EXP, w_a.shape[1]).astype(BF16)


def _cmp_weights(pe, w1, w2):
    n_sub = CMP_LEN // CMP_STRIDE
    w1r = w1.reshape(n_sub, CMP_STRIDE, HEAD_DIM, CMP_HIDDEN)
    eye = jnp.eye(N_KV, dtype=w1.dtype)
    w_ab = jnp.einsum('spdc,gh->pgdshc', w1r, eye).reshape(SUB_W, n_sub * N_KV * CMP_HIDDEN)
    w2x = jnp.einsum('cd,gh->gchd', w2, eye).reshape(N_KV * CMP_HIDDEN, SLAB)
    pe8 = jnp.pad(pe.reshape(1, -1), ((0, 7), (0, 0)))
    return w_ab.astype(BF16), pe8.astype(BF16), w1.astype(BF16), w2x.astype(BF16)


def _importance_matrix(n_cmp, n_cand):
    i = np.arange(n_cmp)[:, None]
    j = np.arange(n_cand)[None, :]
    a = (i >= CMP_RATIO * j - 1) & (i <= CMP_RATIO * j + CMP_RATIO - 1)
    return jnp.asarray(a.astype(np.float32)).astype(BF16)


def _compress_rows(rows_paged, page_table, tail, cw, *, pages_per_step):
    w_ab, pe8, w1b, w2x = cw
    ab = cmp_ab(rows_paged, page_table, w_ab, pages_per_step=pages_per_step)
    ab_tail = None
    if tail is not None:
        one = jnp.zeros((1, 1), jnp.int32)
        ab_tail = cmp_ab(tail.reshape(1, -1, SUB_W), one, w_ab, pages_per_step=1)
        ab_tail = ab_tail.reshape(tail.shape[0], tail.shape[1], -1)
    return cmp_fin(ab, ab_tail, pe8, w1b, w2x)


def kernel(x_prompt, x_sample, cache_cmp_k, cache_cmp_v, cache_sel_k, cache_sel_v, cache_win_k,
           cache_win_v, page_table, ffn1_w_in, ffn1_w_out, ln1_g, ln1_b, w_in, cmp_pe_k, cmp_w1_k,
           cmp_w2_k, cmp_pe_v, cmp_w1_v, cmp_w2_v, gmlp_ln_g, gmlp_ln_b, spatial_w, spatial_b,
           w_branch_a, w_branch_b, w_out, ln2_g, ln2_b, ffn2_w_in, ffn2_w_out, ln3_g, ln3_b):
    nb, t, d = x_prompt.shape
    db, n_new, _ = x_sample.shape
    assert n_new == 1 and t % SEL_TILE == 0
    n_pages = page_table.shape[1]
    past = n_pages * PAGE_SIZE
    n_phys = cache_cmp_k.shape[0]
    m = nb * t

    f1_in, f1_out = ffn1_w_in.astype(BF16), ffn1_w_out.astype(BF16)
    f2_in, f2_out = ffn2_w_in.astype(BF16), ffn2_w_out.astype(BF16)
    proj_w = _split_w_in(w_in)
    wa = _expand_branch_a(w_branch_a)
    wb = w_branch_b.astype(BF16)
    wo = w_out.astype(BF16)
    cw_k = _cmp_weights(cmp_pe_k, cmp_w1_k, cmp_w2_k)
    cw_v = _cmp_weights(cmp_pe_v, cmp_w1_v, cmp_w2_v)

    h = ffn_ln(x_prompt.reshape(m, d), f1_in, f1_out, ln1_g, ln1_b, tm=1024)
    q, kc, vc, ks, vs, kw, vw, kvb, gt, uv, sgab = mixer_proj(h, *proj_w, tm=512)
    sub_per_page = PAGE_SIZE // CMP_STRIDE
    pages_b = t // PAGE_SIZE
    ident = jnp.arange(nb * pages_b, dtype=jnp.int32).reshape(nb, pages_b)
    kcc = _compress_rows(kc.reshape(nb * pages_b, sub_per_page, SUB_W), ident, None, cw_k,
                         pages_per_step=pages_b)
    vcc = _compress_rows(vc.reshape(nb * pages_b, sub_per_page, SUB_W), ident, None, cw_v,
                         pages_per_step=pages_b)
    n_cmp = t // CMP_STRIDE
    imp = _importance_matrix(n_cmp, t // SEL_BLOCK)
    o_a = nsa_prompt(q.reshape(nb, t, Q_EXP), gt.reshape(nb, t, 128), kcc, vcc,
                     kvb.reshape(nb, t, 6 * SLAB), imp)
    sb_tile = jnp.repeat(spatial_b.T, GROUP_CH, axis=1)
    h2 = merge_ln(h, o_a.reshape(m, Q_EXP), uv, sgab, wa, wb, wo, ln2_g, ln2_b, gmlp_ln_g,
                  gmlp_ln_b, spatial_w, sb_tile, tm=256, chunked=True)
    y_prompt = ffn_ln(h2, f2_in, f2_out, ln3_g, ln3_b, tm=1024).reshape(nb, t, d)
    kv4 = lambda a: a.reshape(nb, t, N_KV, HEAD_DIM)
    wkeep = min(WINDOW, t)
    p_outs = (kv4(kc), kv4(vc), kv4(ks), kv4(vs), kv4(kw)[:, -wkeep:], kv4(vw)[:, -wkeep:])

    hs = ffn_ln(x_sample.reshape(db, d), f1_in, f1_out, ln1_g, ln1_b, tm=db)
    (qs, kcs, vcs, kss, vss, kws, vws, _, gts, uvs, sgabs) = mixer_proj(hs, *proj_w, tm=db)
    t_pad = -(-(past + n_new) // SEL_BLOCK) * SEL_BLOCK
    n_tail = (t_pad - past) // CMP_STRIDE
    tail_rows = 16

    def tail_of(new):
        flat = jnp.pad(new, ((0, 0), (0, tail_rows * SUB_W - SLAB)))
        return flat.reshape(db, tail_rows, SUB_W)

    assert n_tail <= tail_rows
    pool3 = lambda p: p.reshape(n_phys, sub_per_page, SUB_W)
    pps = min(64, n_pages)
    kcc_s = _compress_rows(pool3(cache_cmp_k), page_table, tail_of(kcs), cw_k, pages_per_step=pps)
    vcc_s = _compress_rows(pool3(cache_cmp_v), page_table, tail_of(vcs), cw_v, pages_per_step=pps)
    n_sel_s = t_pad // SEL_BLOCK
    n_cand = -(-n_sel_s // 128) * 128
    imp_s = _importance_matrix(kcc_s.shape[1], n_cand)
    q8 = qs.reshape(db, N_HEADS, SLAB)
    o_cmp_s, idx_s = nsa_sample_cmp(q8, kcc_s, vcc_s, imp_s, qpos=past, n_sel=n_sel_s)
    idx_flat = idx_s[:, :N_KV, :SEL_TOP].reshape(db, N_KV * SEL_TOP)
    gt8 = jnp.pad(gts[:, :GATE_W].reshape(db, N_HEADS, 3), ((0, 0), (0, 0), (0, 125)))
    row3 = lambda a: a.reshape(db, 1, SLAB)
    pool_rows = lambda p: p.reshape(n_phys, PAGE_SIZE, SLAB)
    win3 = lambda a: a.reshape(db, -1, SLAB)
    o_as = nsa_sample_sel(page_table, idx_flat, pool_rows(cache_sel_k), pool_rows(cache_sel_v),
                          q8, gt8, o_cmp_s, row3(kss), row3(vss), win3(cache_win_k),
                          win3(cache_win_v), row3(kws), row3(vws), qpos=past)
    sw_row = jnp.repeat(spatial_w[:, 0, 0], GROUP_CH).reshape(1, D_B)
    sb_row = jnp.repeat(spatial_b[:, 0], GROUP_CH).reshape(1, D_B)
    hs2, vn_s = merge_ln(hs, o_as.reshape(db, Q_EXP), uvs, sgabs, wa, wb, wo, ln2_g, ln2_b,
                         gmlp_ln_g, gmlp_ln_b, sw_row, sb_row, tm=db, chunked=False)
    y_sample = ffn_ln(hs2, f2_in, f2_out, ln3_g, ln3_b, tm=db).reshape(db, n_new, d)
    kv4s = lambda a: a.reshape(db, n_new, N_KV, HEAD_DIM)
    wbuf = cache_win_k.shape[1]
    s_win_k = jnp.concatenate([cache_win_k, kv4s(kws)], axis=1)[:, -wbuf:]
    s_win_v = jnp.concatenate([cache_win_v, kv4s(vws)], axis=1)[:, -wbuf:]

    return (y_prompt, y_sample) + p_outs + (kv4s(kcs), kv4s(vcs), kv4s(kss), kv4s(vss),
                                            s_win_k, s_win_v, vn_s.reshape(db, n_new, D_B))
```

```python
import functools

import numpy as np
import jax
import jax.numpy as jnp
from jax import lax
from jax.experimental import pallas as pl
from jax.experimental.pallas import tpu as pltpu

F32 = jnp.float32
BF16 = jnp.bfloat16

D_MODEL = 1024
N_HEADS = 8
HEAD_DIM = 64
N_KV = 2
GROUP = N_HEADS // N_KV
PAGE_SIZE = 128
CMP_LEN = 32
CMP_STRIDE = 16
CMP_HIDDEN = 128
SEL_BLOCK = 64
CMP_RATIO = SEL_BLOCK // CMP_STRIDE
SEL_TOP = 16
WINDOW = 512
Q_BLOCK = 128
FORCE_SCORE = 1e4
D_B = 512
N_GROUPS_B = 4
GROUP_CH = D_B // N_GROUPS_B
CHUNK = 128
D_FF = 2816
DEPTH = 1
ALPHA = (2.0 * DEPTH) ** 0.25
LN_EPS = 1e-5
NEG_INF = -1e30
REMOVED = -3e38
Q_W = N_HEADS * HEAD_DIM
KV_W = N_KV * HEAD_DIM
GATE_W = N_HEADS * 3
SLAB = N_KV * HEAD_DIM
Q_EXP = N_HEADS * SLAB
SUB_W = CMP_STRIDE * SLAB
VMEM_LIMIT = 40 * 1024 * 1024
SEL_TILE = 512


def _params(sem, vmem=VMEM_LIMIT):
    return pltpu.CompilerParams(dimension_semantics=sem, vmem_limit_bytes=vmem)


def _layer_norm(x, g, b):
    mu = jnp.mean(x, axis=-1, keepdims=True)
    xc = x - mu
    var = jnp.mean(xc * xc, axis=-1, keepdims=True)
    return xc * lax.rsqrt(var + LN_EPS) * g + b


def _dot(a, b):
    return jnp.dot(a, b, preferred_element_type=F32)


def _dot_nt(a, b):
    return lax.dot_general(a, b, (((1,), (1,)), ((), ())), preferred_element_type=F32)


def _split3(x):
    hi = x.astype(BF16)
    r1 = x - hi.astype(F32)
    mid = r1.astype(BF16)
    lo = (r1 - mid.astype(F32)).astype(BF16)
    return hi, mid, lo


def _dot_exact_rhs(x, a_bf16):
    hi, mid, lo = _split3(x)
    return _dot(hi, a_bf16) + _dot(mid, a_bf16) + _dot(lo, a_bf16)


def _ffn_ln_kernel(x_ref, wg_ref, wu_ref, wo_ref, g_ref, b_ref, o_ref, xb_ref, acc_ref, *, n_f):
    f = pl.program_id(1)

    @pl.when(f == 0)
    def _():
        xb_ref[...] = x_ref[...].astype(BF16)
        acc_ref[...] = jnp.zeros_like(acc_ref)

    xb = xb_ref[...]
    gate = _dot(xb, wg_ref[...])
    up = _dot(xb, wu_ref[...])
    hid = (gate * jax.nn.sigmoid(gate)) * up
    acc_ref[...] += _dot(hid.astype(BF16), wo_ref[...])

    @pl.when(f == n_f - 1)
    def _():
        y = ALPHA * x_ref[...] + 0.5 * acc_ref[...]
        o_ref[...] = _layer_norm(y, g_ref[...], b_ref[...])


def ffn_ln(x, w_in_b, w_out_b, g, b, *, tm, tf=256):
    m, d = x.shape
    d_ff = w_out_b.shape[0]
    n_f = d_ff // tf
    return pl.pallas_call(
        functools.partial(_ffn_ln_kernel, n_f=n_f),
        grid=(m // tm, n_f),
        in_specs=[
            pl.BlockSpec((tm, d), lambda i, f: (i, 0)),
            pl.BlockSpec((d, tf), lambda i, f: (0, f)),
            pl.BlockSpec((d, tf), lambda i, f: (0, f + n_f)),
            pl.BlockSpec((tf, d), lambda i, f: (f, 0)),
            pl.BlockSpec((1, d), lambda i, f: (0, 0)),
            pl.BlockSpec((1, d), lambda i, f: (0, 0)),
        ],
        out_specs=pl.BlockSpec((tm, d), lambda i, f: (i, 0)),
        out_shape=jax.ShapeDtypeStruct((m, d), F32),
        scratch_shapes=[pltpu.VMEM((tm, d), BF16), pltpu.VMEM((tm, d), F32)],
        compiler_params=_params(("parallel", "arbitrary")),
        name="ffn_ln",
    )(x, w_in_b, w_in_b, w_out_b, g.reshape(1, d), b.reshape(1, d))


def _mixer_proj_kernel(h_ref, wq_ref, wkv_ref, wgt_ref, wuv_ref, wgab_ref,
                       q_ref, kc_ref, vc_ref, ks_ref, vs_ref, kw_ref, vw_ref,
                       kvb_ref, gt_ref, uv_ref, sgab_ref):
    hb = h_ref[...].astype(BF16)
    q_ref[...] = (_dot(hb, wq_ref[...]) * (HEAD_DIM ** -0.5)).astype(BF16)
    kv = _dot(hb, wkv_ref[...])
    for j, ref in enumerate((kc_ref, vc_ref, ks_ref, vs_ref, kw_ref, vw_ref)):
        ref[...] = kv[:, j * SLAB:(j + 1) * SLAB]
    kvb_ref[...] = kv.astype(BF16)
    gt_ref[...] = jax.nn.sigmoid(_dot(hb, wgt_ref[...]))
    uv_ref[...] = jax.nn.gelu(_dot(hb, wuv_ref[...]))
    sgab_ref[...] = jax.nn.sigmoid(_dot(hb, wgab_ref[...])).astype(BF16)


def mixer_proj(h, wq, wkv, wgt, wuv, wgab, *, tm):
    m, d = h.shape
    row = lambda n: pl.BlockSpec((tm, n), lambda i: (i, 0))
    full = lambda w: pl.BlockSpec(w.shape, lambda i: (0, 0))
    out_shape = (
        [jax.ShapeDtypeStruct((m, Q_EXP), BF16)]
        + [jax.ShapeDtypeStruct((m, SLAB), F32)] * 6
        + [jax.ShapeDtypeStruct((m, 6 * SLAB), BF16),
           jax.ShapeDtypeStruct((m, 128), F32),
           jax.ShapeDtypeStruct((m, 2 * D_B), F32),
           jax.ShapeDtypeStruct((m, 2 * D_MODEL), BF16)])
    out_specs = ([row(Q_EXP)] + [row(SLAB)] * 6
                 + [row(6 * SLAB), row(128), row(2 * D_B), row(2 * D_MODEL)])
    return pl.pallas_call(
        _mixer_proj_kernel,
        grid=(m // tm,),
        in_specs=[row(d), full(wq), full(wkv), full(wgt), full(wuv), full(wgab)],
        out_specs=out_specs,
        out_shape=out_shape,
        compiler_params=_params(("parallel",)),
        name="mixer_proj",
    )(h, wq, wkv, wgt, wuv, wgab)


def _mixer_proj_t_kernel(h_ref, wq_ref, wkv_ref, wgt_ref, wuv_ref, wgab_ref,
                         qt_ref, kct_ref, vct_ref, kst_ref, vst_ref, kwt_ref, vwt_ref,
                         kcb_ref, vcb_ref, ksb_ref, kwb_ref, vstb_ref, vwtb_ref,
                         gtt_ref, uv_ref, sgab_ref):
    hb = h_ref[...].astype(BF16)
    q = _dot(hb, wq_ref[...]) * (HEAD_DIM ** -0.5)
    for h in range(N_HEADS):
        qt_ref[h * SLAB:(h + 1) * SLAB, :] = q[:, h * SLAB:(h + 1) * SLAB].T.astype(BF16)
    kv = _dot(hb, wkv_ref[...])
    slabs = [kv[:, j * SLAB:(j + 1) * SLAB] for j in range(6)]
    for slab, ref in zip(slabs, (kct_ref, vct_ref, kst_ref, vst_ref, kwt_ref, vwt_ref)):
        ref[...] = slab.T
    kcb_ref[...] = slabs[0].astype(BF16)
    vcb_ref[...] = slabs[1].astype(BF16)
    ksb_ref[...] = slabs[2].astype(BF16)
    kwb_ref[...] = slabs[4].astype(BF16)
    vstb_ref[...] = slabs[3].T.astype(BF16)
    vwtb_ref[...] = slabs[5].T.astype(BF16)
    gtt_ref[...] = jax.nn.sigmoid(_dot(hb, wgt_ref[...])).T
    uv_ref[...] = jax.nn.gelu(_dot(hb, wuv_ref[...]))
    sgab_ref[...] = jax.nn.sigmoid(_dot(hb, wgab_ref[...])).astype(BF16)


def mixer_proj_t(h, wq, wkv, wgt, wuv, wgab, *, nb, tm):
    m, d = h.shape
    t = m // nb
    per_b = t // tm
    row = lambda n: pl.BlockSpec((tm, n), lambda i: (i, 0))
    col = lambda n: pl.BlockSpec((None, n, tm), lambda i: (i // per_b, 0, i % per_b))
    full = lambda w: pl.BlockSpec(w.shape, lambda i: (0, 0))
    sds = jax.ShapeDtypeStruct
    out_shape = ([sds((nb, Q_EXP, t), BF16)] + [sds((nb, SLAB, t), F32)] * 6
                 + [sds((m, SLAB), BF16)] * 4 + [sds((nb, SLAB, t), BF16)] * 2
                 + [sds((nb, 128, t), F32), sds((m, 2 * D_B), F32), sds((m, 2 * D_MODEL), BF16)])
    out_specs = ([col(Q_EXP)] + [col(SLAB)] * 6 + [row(SLAB)] * 4 + [col(SLAB)] * 2
                 + [col(128), row(2 * D_B), row(2 * D_MODEL)])
    return pl.pallas_call(
        _mixer_proj_t_kernel,
        grid=(m // tm,),
        in_specs=[row(d), full(wq), full(wkv), full(wgt), full(wuv), full(wgab)],
        out_specs=out_specs,
        out_shape=out_shape,
        compiler_params=_params(("parallel",)),
        name="mixer_proj_t",
    )(h, wq, wkv, wgt, wuv, wgab)


def _cmp_ab_kernel(pt_ref, *refs, n_pages):
    del pt_ref
    page_refs = refs[:n_pages]
    w_ref = refs[n_pages]
    o_ref = refs[n_pages + 1]
    if n_pages == 1:
        x = page_refs[0][...]
    else:
        x = jnp.concatenate([r[...] for r in page_refs], axis=0)
    o_ref[...] = _dot(x.astype(BF16), w_ref[...])


def cmp_ab(pool, page_table, w_ab, *, pages_per_step):
    nb, n_pages = page_table.shape
    rows = pool.shape[1]
    pp = pages_per_step
    n_steps = n_pages // pp

    def page_spec(k):
        return pl.BlockSpec((None, rows, SUB_W), lambda b, s, pt: (pt[b, s * pp + k], 0, 0))

    grid_spec = pltpu.PrefetchScalarGridSpec(
        num_scalar_prefetch=1,
        grid=(nb, n_steps),
        in_specs=[page_spec(k) for k in range(pp)]
        + [pl.BlockSpec(w_ab.shape, lambda b, s, pt: (0, 0))],
        out_specs=pl.BlockSpec((None, pp * rows, w_ab.shape[1]), lambda b, s, pt: (b, s, 0)),
    )
    return pl.pallas_call(
        functools.partial(_cmp_ab_kernel, n_pages=pp),
        grid_spec=grid_spec,
        out_shape=jax.ShapeDtypeStruct((nb, n_pages * rows, w_ab.shape[1]), F32),
        compiler_params=_params(("parallel", "arbitrary")),
        name="cmp_ab",
    )(page_table, *([pool] * pp), w_ab)


def _cmp_fin_kernel(*refs, has_tail):
    if has_tail:
        ab_ref, tail_ref, pe_ref, w1_ref, w2_ref, o_ref = refs
    else:
        ab_ref, pe_ref, w1_ref, w2_ref, o_ref = refs
    hw = 2 * CMP_HIDDEN
    n = ab_ref.shape[0]
    pe_h = _dot(pe_ref[...], w1_ref[...])[0:1, :]
    bias = jnp.concatenate([pe_h, pe_h], axis=1)
    w2 = w2_ref[...]
    first = ab_ref[:, 0:hw]
    second = pltpu.roll(ab_ref[:, hw:2 * hw], n - 1, 0)
    if has_tail:
        t_first = tail_ref[:, 0:hw]
        t_second = tail_ref[:, hw:2 * hw]
        nt = tail_ref.shape[0]
        is_last = lax.broadcasted_iota(jnp.int32, (n, 1), 0) == n - 1
        second = jnp.where(is_last, t_second[0:1, :], second)
        t_hid = t_first + pltpu.roll(t_second, nt - 1, 0) + bias
        o_ref[n:n + nt, :] = _dot(jax.nn.gelu(t_hid).astype(BF16), w2).astype(BF16)
        n_out = o_ref.shape[0]
        if n_out > n + nt:
            o_ref[n + nt:n_out, :] = jnp.zeros((n_out - n - nt, SLAB), BF16)
    hid = first + second + bias
    o_ref[0:n, :] = _dot(jax.nn.gelu(hid).astype(BF16), w2).astype(BF16)


def cmp_fin(ab, tail, pe8, w1b, w2x):
    nb, n, w = ab.shape
    has_tail = tail is not None
    nt = tail.shape[1] if has_tail else 0
    full = lambda a: pl.BlockSpec(a.shape, lambda b: (0, 0))
    in_specs = [pl.BlockSpec((None, n, w), lambda b: (b, 0, 0))]
    args = [ab]
    if has_tail:
        in_specs.append(pl.BlockSpec((None, nt, w), lambda b: (b, 0, 0)))
        args.append(tail)
    in_specs += [full(pe8), full(w1b), full(w2x)]
    args += [pe8, w1b, w2x]
    n_out = -(-(n + nt) // 128) * 128 if has_tail else n
    return pl.pallas_call(
        functools.partial(_cmp_fin_kernel, has_tail=has_tail),
        grid=(nb,),
        in_specs=in_specs,
        out_specs=pl.BlockSpec((None, n_out, SLAB), lambda b: (b, 0, 0)),
        out_shape=jax.ShapeDtypeStruct((nb, n_out, SLAB), BF16),
        compiler_params=_params(("parallel",)),
        name="cmp_fin",
    )(*args)


def _slope_of_head(h):
    return 2.0 ** (-(h + 1))


def _slope_column():
    head = lax.broadcasted_iota(jnp.int32, (N_HEADS, 1), 0)
    slope = jnp.zeros((N_HEADS, 1), F32)
    for h in range(N_HEADS):
        slope = jnp.where(head == h, _slope_of_head(h), slope)
    return slope


def _top_k_mask_t(score_t, k):
    n = score_t.shape[0]
    cand_idx = lax.broadcasted_iota(jnp.int32, score_t.shape, 0).astype(F32)

    def body(_, carry):
        s, sel = carry
        m = jnp.max(s, axis=0, keepdims=True)
        first = jnp.min(jnp.where(s == m, cand_idx, float(n)), axis=0, keepdims=True)
        pick = cand_idx == first
        return jnp.where(pick, REMOVED, s), jnp.where(pick, 1.0, sel)

    _, sel = lax.fori_loop(0, k, body, (score_t, jnp.zeros_like(score_t)), unroll=True)
    return sel


def _softmax_masked(s, mask):
    s = jnp.where(mask, s, NEG_INF)
    m = jnp.max(s, axis=-1, keepdims=True)
    e = jnp.exp(s - m)
    return jnp.where(mask, e / jnp.sum(e, axis=-1, keepdims=True), 0.0)


def _softmax_masked_t(s, mask):
    s = jnp.where(mask, s, NEG_INF)
    m = jnp.max(s, axis=0, keepdims=True)
    e = jnp.exp(s - m)
    return jnp.where(mask, e / jnp.sum(e, axis=0, keepdims=True), 0.0)


def _nsa_prompt_kernel(qt_ref, gtt_ref, kc_ref, vct_ref, ks_ref, vst_ref, kw_ref, vwt_ref, impt_ref,
                       o_ref, selt_ref):
    qb = pl.program_id(1)
    s0 = qb * Q_BLOCK
    nq = Q_BLOCK
    hq = GROUP * nq
    nc = kc_ref.shape[0]
    n_sel = ks_ref.shape[0] // SEL_BLOCK
    tk = SEL_TILE
    assert HEAD_DIM == 64 and SEL_BLOCK == 64 and Q_BLOCK == 128

    lane = lax.broadcasted_iota(jnp.int32, (1, hq), 1)
    qpos = s0 + (lane & (nq - 1))
    qpos_f = qpos.astype(F32)
    qpos_q = s0 + lax.broadcasted_iota(jnp.int32, (1, nq), 1)
    row_half = lax.broadcasted_iota(jnp.int32, (SLAB, 1), 0) >> 6

    c_end = lax.broadcasted_iota(jnp.int32, (nc, 1), 0) * CMP_STRIDE + (CMP_LEN - 1)
    dist_c = qpos - c_end
    mask_c = dist_c >= 0
    dist_c_f = dist_c.astype(F32)

    j_idx = lax.broadcasted_iota(jnp.int32, (n_sel, 1), 0)
    cur = qpos_q >> 6
    forced = (j_idx == 0) | (j_idx == cur) | (j_idx == cur - 1)
    valid = (j_idx * SEL_BLOCK) <= qpos_q

    w0 = pl.multiple_of(jnp.maximum(s0 - WINDOW, 0), Q_BLOCK)
    n_win = WINDOW + nq
    wpos = w0 + lax.broadcasted_iota(jnp.int32, (n_win, 1), 0)
    dist_w = qpos - wpos
    mask_w = (dist_w >= 0) & (dist_w < WINDOW)
    dist_w_f = dist_w.astype(F32)
    kw = kw_ref[pl.ds(w0, n_win), :]
    vwt = vwt_ref[:, pl.ds(w0, n_win)]

    k_loc = lax.broadcasted_iota(jnp.int32, (tk, 1), 0)
    k_loc_f = k_loc.astype(F32)
    n_full = s0 // tk
    n_tiles = (s0 + nq + tk - 1) // tk
    blocks_per_tile = tk // SEL_BLOCK

    for g in range(N_KV):
        heads = [g * GROUP + r for r in range(GROUP)]
        slope = jnp.zeros((1, hq), F32)
        for r, h in enumerate(heads):
            slope = jnp.where((lane >> 7) == r, _slope_of_head(h), slope)
        qt = jnp.concatenate([qt_ref[h * SLAB:(h + 1) * SLAB, :] for h in heads], axis=1)

        s = _dot(kc_ref[...], qt) - slope * dist_c_f
        p = _softmax_masked_t(s, mask_c)
        o_cmp = _dot(vct_ref[...], p.astype(BF16))
        p_sum = p[:, 0:nq]
        for r in range(1, GROUP):
            p_sum = p_sum + p[:, r * nq:(r + 1) * nq]

        hi, mid, lo = _split3(p_sum)
        imp_t = impt_ref[...]
        p_slc = _dot(imp_t, hi) + _dot(imp_t, mid) + _dot(imp_t, lo)
        score = jnp.where(valid, jnp.where(forced, FORCE_SCORE, p_slc), NEG_INF)
        selt_ref[...] = _top_k_mask_t(score, min(SEL_TOP, n_sel))

        rel_bias = slope * k_loc_f

        def sel_tile(t, carry, diagonal):
            m_old, l_old, acc_old = carry
            kt = pl.multiple_of(t * tk, tk)
            k = ks_ref[pl.ds(kt, tk), :]
            vt = vst_ref[:, pl.ds(kt, tk)]
            picked = jnp.concatenate(
                [jnp.broadcast_to(selt_ref[pl.ds(t * blocks_per_tile + jj, 1), :], (SEL_BLOCK, nq))
                 for jj in range(blocks_per_tile)], axis=0) > 0.5
            if diagonal:
                picked = picked & ((kt + k_loc) <= qpos_q)
            s_rel = _dot(k, qt) + rel_bias
            shift = slope * (kt.astype(F32) - qpos_f)
            s_rel = jnp.concatenate(
                [jnp.where(picked, s_rel[:, r * nq:(r + 1) * nq], NEG_INF) for r in range(GROUP)],
                axis=1)
            m_new = jnp.maximum(m_old, jnp.max(s_rel, axis=0, keepdims=True) + shift)
            alpha = jnp.exp(m_old - m_new)
            p_t = jnp.exp(s_rel - (m_new - shift))
            l_new = alpha * l_old + jnp.sum(p_t, axis=0, keepdims=True)
            acc_new = alpha * acc_old + _dot(vt, p_t.astype(BF16))
            return m_new, l_new, acc_new

        init = (jnp.full((1, hq), NEG_INF, F32), jnp.zeros((1, hq), F32), jnp.zeros((SLAB, hq), F32))
        state = lax.fori_loop(0, n_full, functools.partial(sel_tile, diagonal=False), init)
        _, l_sel, acc_sel = lax.fori_loop(n_full, n_tiles,
                                          functools.partial(sel_tile, diagonal=True), state)
        o_sel = acc_sel / l_sel

        s = _dot(kw, qt) - slope * dist_w_f
        o_win = _dot(vwt, _softmax_masked_t(s, mask_w).astype(BF16))

        for r, h in enumerate(heads):
            cols = slice(r * nq, (r + 1) * nq)
            o = (gtt_ref[3 * h:3 * h + 1, :] * o_cmp[:, cols]
                 + gtt_ref[3 * h + 1:3 * h + 2, :] * o_sel[:, cols]
                 + gtt_ref[3 * h + 2:3 * h + 3, :] * o_win[:, cols])
            o_ref[:, h * SLAB:(h + 1) * SLAB] = jnp.where(row_half == g, o, 0.0).T.astype(BF16)


def nsa_prompt(qt, gtt, kc, vct, ks, vst, kw, vwt, imp_t):
    nb, _, t = qt.shape
    nc = kc.shape[1]
    rows = pl.BlockSpec((None, t, SLAB), lambda b, i: (b, 0, 0))
    cols = pl.BlockSpec((None, SLAB, t), lambda b, i: (b, 0, 0))
    return pl.pallas_call(
        _nsa_prompt_kernel,
        grid=(nb, t // Q_BLOCK),
        in_specs=[
            pl.BlockSpec((None, Q_EXP, Q_BLOCK), lambda b, i: (b, 0, i)),
            pl.BlockSpec((None, 128, Q_BLOCK), lambda b, i: (b, 0, i)),
            pl.BlockSpec((None, nc, SLAB), lambda b, i: (b, 0, 0)),
            pl.BlockSpec((None, SLAB, nc), lambda b, i: (b, 0, 0)),
            rows, cols, rows, cols,
            pl.BlockSpec(imp_t.shape, lambda b, i: (0, 0)),
        ],
        out_specs=pl.BlockSpec((None, Q_BLOCK, Q_EXP), lambda b, i: (b, i, 0)),
        out_shape=jax.ShapeDtypeStruct((nb, t, Q_EXP), BF16),
        scratch_shapes=[pltpu.VMEM((t // SEL_BLOCK, Q_BLOCK), F32)],
        compiler_params=_params(("parallel", "arbitrary")),
        name="nsa_prompt",
    )(qt, gtt, kc, vct, ks, vst, kw, vwt, imp_t)


def _nsa_sample_cmp_kernel(q_ref, kc_ref, vc_ref, imp_ref, oc_ref, idx_ref, *, qpos, n_sel):
    nc = kc_ref.shape[0]
    n_cand = imp_ref.shape[1]
    q = q_ref[...]
    head = lax.broadcasted_iota(jnp.int32, (N_HEADS, 1), 0)
    slope = _slope_column()
    c_end = lax.broadcasted_iota(jnp.int32, (1, nc), 1) * CMP_STRIDE + (CMP_LEN - 1)
    dist = qpos - c_end
    mask = dist >= 0
    s = _dot_nt(q, kc_ref[...]) - slope * dist.astype(F32)
    p = _softmax_masked(s, mask)
    oc_ref[...] = _dot(p.astype(BF16), vc_ref[...])

    p_sum = jnp.zeros_like(p)
    for g in range(N_KV):
        grp = jnp.sum(p[g * GROUP:(g + 1) * GROUP], axis=0, keepdims=True)
        p_sum = jnp.where(head == g, grp, p_sum)
    p_slc = _dot_exact_rhs(p_sum, imp_ref[...])

    j_idx = lax.broadcasted_iota(jnp.int32, (1, n_cand), 1)
    cur = qpos // SEL_BLOCK
    forced = (j_idx == 0) | (j_idx == cur) | (j_idx == cur - 1)
    valid = (j_idx * SEL_BLOCK) <= qpos
    score = jnp.where(valid, jnp.where(forced, FORCE_SCORE, p_slc), NEG_INF)
    score = jnp.where(j_idx < n_sel, score, REMOVED)

    cand = lax.broadcasted_iota(jnp.int32, score.shape, 1).astype(F32)
    out_lane = lax.broadcasted_iota(jnp.int32, (N_HEADS, 128), 1)

    def body(k, carry):
        sc, out = carry
        m = jnp.max(sc, axis=1, keepdims=True)
        first = jnp.min(jnp.where(sc == m, cand, float(n_cand)), axis=1, keepdims=True)
        return (jnp.where(cand == first, REMOVED, sc),
                jnp.where(out_lane == k, first.astype(jnp.int32), out))

    _, out = lax.fori_loop(0, SEL_TOP, body, (score, jnp.zeros((N_HEADS, 128), jnp.int32)),
                           unroll=True)
    idx_ref[...] = out


def nsa_sample_cmp(q8, kc, vc, imp, *, qpos, n_sel):
    nb = q8.shape[0]
    nc = kc.shape[1]
    return pl.pallas_call(
        functools.partial(_nsa_sample_cmp_kernel, qpos=qpos, n_sel=n_sel),
        grid=(nb,),
        in_specs=[
            pl.BlockSpec((None, N_HEADS, SLAB), lambda b: (b, 0, 0)),
            pl.BlockSpec((None, nc, SLAB), lambda b: (b, 0, 0)),
            pl.BlockSpec((None, nc, SLAB), lambda b: (b, 0, 0)),
            pl.BlockSpec(imp.shape, lambda b: (0, 0)),
        ],
        out_specs=[pl.BlockSpec((None, N_HEADS, SLAB), lambda b: (b, 0, 0)),
                   pl.BlockSpec((None, N_HEADS, 128), lambda b: (b, 0, 0))],
        out_shape=[jax.ShapeDtypeStruct((nb, N_HEADS, SLAB), F32),
                   jax.ShapeDtypeStruct((nb, N_HEADS, 128), jnp.int32)],
        compiler_params=_params(("parallel",)),
        name="nsa_sample_cmp",
    )(q8, kc, vc, imp)


def _nsa_sample_sel_kernel(pt_ref, idx_ref, *refs, qpos, n_past_blocks):
    del pt_ref
    n_pg = N_KV * SEL_TOP
    k_pages = refs[:n_pg]
    v_pages = refs[n_pg:2 * n_pg]
    (q_ref, gt_ref, oc_ref, kn_ref, vn_ref, wk_ref, wv_ref, wkn_ref, wvn_ref,
     o_ref) = refs[2 * n_pg:]
    b = pl.program_id(0)
    q = q_ref[...]
    head = lax.broadcasted_iota(jnp.int32, (N_HEADS, 1), 0)
    slope = _slope_column()
    grp_of_row = head >> 2
    assert GROUP == 4 and HEAD_DIM == 64 and SEL_BLOCK == 64
    lane_half = lax.broadcasted_iota(jnp.int32, (1, SLAB), 1) >> 6
    blocks_per_page = PAGE_SIZE // SEL_BLOCK
    n_keys = SEL_TOP * SEL_BLOCK
    key_slot = lax.broadcasted_iota(jnp.int32, (1, n_keys), 1) >> 6
    key_off = lax.broadcasted_iota(jnp.int32, (1, n_keys), 1) & (SEL_BLOCK - 1)
    qf = q.astype(F32)

    s_new = jnp.sum(qf * kn_ref[...], axis=1, keepdims=True)

    o_sel = jnp.zeros((N_HEADS, SLAB), F32)
    for g in range(N_KV):
        ks, vs = [], []
        start = jnp.zeros((1, n_keys), jnp.int32)
        for k in range(SEL_TOP):
            j = idx_ref[b, g * SEL_TOP + k]
            off = pl.multiple_of((j % blocks_per_page) * SEL_BLOCK, SEL_BLOCK)
            ks.append(k_pages[g * SEL_TOP + k][pl.ds(off, SEL_BLOCK), :])
            vs.append(v_pages[g * SEL_TOP + k][pl.ds(off, SEL_BLOCK), :])
            start = jnp.where(key_slot == k, j * SEL_BLOCK, start)
        k_all = jnp.concatenate(ks, axis=0).astype(BF16)
        v_all = jnp.concatenate(vs, axis=0).astype(BF16)
        spos = start + key_off
        dist = qpos - spos
        mask = (dist >= 0) & (start < n_past_blocks * SEL_BLOCK)
        s = _dot_nt(q, k_all) - slope * dist.astype(F32)
        s = jnp.where(mask, s, NEG_INF)
        m = jnp.maximum(jnp.max(s, axis=-1, keepdims=True), s_new)
        e = jnp.exp(s - m)
        e_new = jnp.exp(s_new - m)
        l = jnp.sum(e, axis=-1, keepdims=True) + e_new
        o_g = (_dot(e.astype(BF16), v_all) + e_new * vn_ref[...]) / l
        o_sel = jnp.where(grp_of_row == g, o_g, o_sel)

    n_win = wk_ref.shape[0]
    dist_w = n_win - lax.broadcasted_iota(jnp.int32, (1, n_win), 1)
    mask_w = (dist_w >= 0) & (dist_w < WINDOW)
    s_w = _dot_nt(q, wk_ref[...].astype(BF16)) - slope * dist_w.astype(F32)
    s_w = jnp.where(mask_w, s_w, NEG_INF)
    s_wn = jnp.sum(qf * wkn_ref[...], axis=1, keepdims=True)
    m_w = jnp.maximum(jnp.max(s_w, axis=-1, keepdims=True), s_wn)
    e_w = jnp.exp(s_w - m_w)
    e_wn = jnp.exp(s_wn - m_w)
    l_w = jnp.sum(e_w, axis=-1, keepdims=True) + e_wn
    o_win = (_dot(e_w.astype(BF16), wv_ref[...].astype(BF16)) + e_wn * wvn_ref[...]) / l_w

    gt = gt_ref[...]
    o = gt[:, 0:1] * oc_ref[...] + gt[:, 1:2] * o_sel + gt[:, 2:3] * o_win
    o_ref[...] = jnp.where(lane_half == grp_of_row, o, 0.0).astype(BF16)


def nsa_sample_sel(page_table, idx, pool_k, pool_v, q8, gt8, o_cmp, k_new, v_new,
                   win_k, win_v, wk_new, wv_new, *, qpos):
    nb, n_pages = page_table.shape
    n_win = win_k.shape[1]
    blocks_per_page = PAGE_SIZE // SEL_BLOCK

    def page_spec(i):
        def index_map(b, pt, ix):
            page = jnp.minimum(ix[b, i] // blocks_per_page, n_pages - 1)
            return (pt[b, page], 0, 0)
        return pl.BlockSpec((None, PAGE_SIZE, SLAB), index_map)

    per_b = lambda r: pl.BlockSpec((None, r, SLAB), lambda b, pt, ix: (b, 0, 0))
    n_pg = N_KV * SEL_TOP
    grid_spec = pltpu.PrefetchScalarGridSpec(
        num_scalar_prefetch=2,
        grid=(nb,),
        in_specs=[page_spec(i) for i in range(n_pg)] * 2
        + [per_b(N_HEADS), per_b(N_HEADS), per_b(N_HEADS), per_b(1), per_b(1),
           per_b(n_win), per_b(n_win), per_b(1), per_b(1)],
        out_specs=per_b(N_HEADS),
    )
    return pl.pallas_call(
        functools.partial(_nsa_sample_sel_kernel, qpos=qpos, n_past_blocks=n_pages * blocks_per_page),
        grid_spec=grid_spec,
        out_shape=jax.ShapeDtypeStruct((nb, N_HEADS, SLAB), BF16),
        compiler_params=_params(("arbitrary",)),
        name="nsa_sample_sel",
    )(page_table, idx, *([pool_k] * n_pg), *([pool_v] * n_pg), q8, gt8, o_cmp, k_new, v_new,
      win_k, win_v, wk_new, wv_new)


def _merge_ln_kernel(*refs, chunked):
    if chunked:
        (h_ref, oa_ref, uv_ref, sg_ref, wa_ref, wb_ref, wo_ref, g2_ref, b2_ref, lg_ref, lb_ref,
         sw_ref, sb_ref, o_ref) = refs
    else:
        (h_ref, oa_ref, uv_ref, sg_ref, wa_ref, wb_ref, wo_ref, g2_ref, b2_ref, lg_ref, lb_ref,
         sw_ref, sb_ref, o_ref, vn_ref) = refs
    tm = h_ref.shape[0]
    u = uv_ref[:, 0:D_B]
    vn = _layer_norm(uv_ref[:, D_B:2 * D_B], lg_ref[...], lb_ref[...])
    if chunked:
        row = lax.broadcasted_iota(jnp.int32, (CHUNK, CHUNK), 0)
        col = lax.broadcasted_iota(jnp.int32, (CHUNK, CHUNK), 1)
        vnb = vn.astype(BF16)
        chunks = []
        for c in range(tm // CHUNK):
            parts = []
            for hg in range(N_GROUPS_B):
                w = jnp.where(col <= row, sw_ref[hg], 0.0).astype(BF16)
                parts.append(_dot(w, vnb[c * CHUNK:(c + 1) * CHUNK, hg * GROUP_CH:(hg + 1) * GROUP_CH]))
            chunks.append(jnp.concatenate(parts, axis=1) + sb_ref[...])
        s = jnp.concatenate(chunks, axis=0)
    else:
        s = vn * sw_ref[...] + sb_ref[...]
        vn_ref[...] = vn
    z = u * s
    branch_a = _dot(oa_ref[...], wa_ref[...])
    branch_b = _dot(z.astype(BF16), wb_ref[...])
    merged = (sg_ref[:, 0:D_MODEL].astype(F32) * branch_a
              + sg_ref[:, D_MODEL:2 * D_MODEL].astype(F32) * branch_b)
    y = ALPHA * h_ref[...] + _dot(merged.astype(BF16), wo_ref[...])
    o_ref[...] = _layer_norm(y, g2_ref[...], b2_ref[...])


def merge_ln(h, oa, uv, sgab, wa, wb, wo, g2, b2, lg, lb, sw, sb, *, tm, chunked):
    m, d = h.shape
    row = lambda n: pl.BlockSpec((tm, n), lambda i: (i, 0))
    full = lambda a: pl.BlockSpec(a.shape, lambda i: (0,) * a.ndim)
    vec = lambda a: a.reshape(1, -1)
    g2, b2, lg, lb = vec(g2), vec(b2), vec(lg), vec(lb)
    out_shape = [jax.ShapeDtypeStruct((m, d), F32)]
    out_specs = [row(d)]
    if not chunked:
        out_shape.append(jax.ShapeDtypeStruct((m, D_B), F32))
        out_specs.append(row(D_B))
    res = pl.pallas_call(
        functools.partial(_merge_ln_kernel, chunked=chunked),
        grid=(m // tm,),
        in_specs=[row(d), row(Q_EXP), row(2 * D_B), row(2 * D_MODEL), full(wa), full(wb), full(wo),
                  full(g2), full(b2), full(lg), full(lb), full(sw), full(sb)],
        out_specs=out_specs,
        out_shape=out_shape,
        compiler_params=_params(("parallel",)),
        name="merge_ln",
    )(h, oa, uv, sgab, wa, wb, wo, g2, b2, lg, lb, sw, sb)
    return res if not chunked else res[0]


def _group_mask():
    return (np.arange(N_HEADS)[:, None] // GROUP == np.arange(N_KV)[None, :]).astype(np.float32)


def _split_w_in(w_in):
    sizes = (Q_W,) + (KV_W,) * 6 + (GATE_W, D_B, D_B, D_MODEL, D_MODEL)
    cuts = np.cumsum(sizes)[:-1].tolist()
    q, kc, vc, ks, vs, kw, vw, g, u, v, ga, gb = jnp.split(w_in, cuts, axis=1)
    d = w_in.shape[0]
    gm = jnp.asarray(_group_mask())
    wq = (q.reshape(d, N_HEADS, 1, HEAD_DIM) * gm[None, :, :, None]).reshape(d, Q_EXP)
    wkv = jnp.concatenate([kc, vc, ks, vs, kw, vw], axis=1)
    wgt = jnp.pad(g, ((0, 0), (0, 128 - GATE_W)))
    wuv = jnp.concatenate([u, v], axis=1)
    wgab = jnp.concatenate([ga, gb], axis=1)
    return tuple(w.astype(BF16) for w in (wq, wkv, wgt, wuv, wgab))


def _expand_branch_a(w_a):
    gm = jnp.asarray(_group_mask())
    w = w_a.reshape(N_HEADS, 1, HEAD_DIM, w_a.shape[1]) * gm[:, :, None, None]
    return w.reshape(Q_EXP, w_a.shape[1]).astype(BF16)


def _cmp_weights(pe, w1, w2):
    n_sub = CMP_LEN // CMP_STRIDE
    w1r = w1.reshape(n_sub, CMP_STRIDE, HEAD_DIM, CMP_HIDDEN)
    eye = jnp.eye(N_KV, dtype=w1.dtype)
    w_ab = jnp.einsum('spdc,gh->pgdshc', w1r, eye).reshape(SUB_W, n_sub * N_KV * CMP_HIDDEN)
    w2x = jnp.einsum('cd,gh->gchd', w2, eye).reshape(N_KV * CMP_HIDDEN, SLAB)
    pe8 = jnp.pad(pe.reshape(1, -1), ((0, 7), (0, 0)))
    return w_ab.astype(BF16), pe8.astype(BF16), w1.astype(BF16), w2x.astype(BF16)


def _importance_matrix(n_cmp, n_cand):
    i = np.arange(n_cmp)[:, None]
    j = np.arange(n_cand)[None, :]
    a = (i >= CMP_RATIO * j - 1) & (i <= CMP_RATIO * j + CMP_RATIO - 1)
    return jnp.asarray(a.astype(np.float32)).astype(BF16)


def _compress_rows(rows, cw):
    w_ab, pe8, w1b, w2x = cw
    nb = rows.shape[0]
    ident = jnp.arange(nb, dtype=jnp.int32).reshape(nb, 1)
    return cmp_fin(cmp_ab(rows, ident, w_ab, pages_per_step=1), None, pe8, w1b, w2x)


def _compress_paged(pool, page_table, tail, cw, *, pages_per_step):
    w_ab, pe8, w1b, w2x = cw
    ab = cmp_ab(pool, page_table, w_ab, pages_per_step=pages_per_step)
    one = jnp.zeros((1, 1), jnp.int32)
    ab_tail = cmp_ab(tail.reshape(1, -1, SUB_W), one, w_ab, pages_per_step=1)
    ab_tail = ab_tail.reshape(tail.shape[0], tail.shape[1], -1)
    return cmp_fin(ab, ab_tail, pe8, w1b, w2x)


def _position_major(a_t):
    n, _, p = a_t.shape
    return jnp.transpose(a_t.reshape(n, N_KV, HEAD_DIM, p), (0, 3, 1, 2))


def kernel(x_prompt, x_sample, cache_cmp_k, cache_cmp_v, cache_sel_k, cache_sel_v, cache_win_k,
           cache_win_v, page_table, ffn1_w_in, ffn1_w_out, ln1_g, ln1_b, w_in, cmp_pe_k, cmp_w1_k,
           cmp_w2_k, cmp_pe_v, cmp_w1_v, cmp_w2_v, gmlp_ln_g, gmlp_ln_b, spatial_w, spatial_b,
           w_branch_a, w_branch_b, w_out, ln2_g, ln2_b, ffn2_w_in, ffn2_w_out, ln3_g, ln3_b):
    nb, t, d = x_prompt.shape
    db, n_new, _ = x_sample.shape
    assert n_new == 1 and t % SEL_TILE == 0
    n_pages = page_table.shape[1]
    past = n_pages * PAGE_SIZE
    n_phys = cache_cmp_k.shape[0]
    m = nb * t

    f1_in, f1_out = ffn1_w_in.astype(BF16), ffn1_w_out.astype(BF16)
    f2_in, f2_out = ffn2_w_in.astype(BF16), ffn2_w_out.astype(BF16)
    proj_w = _split_w_in(w_in)
    wa = _expand_branch_a(w_branch_a)
    wb = w_branch_b.astype(BF16)
    wo = w_out.astype(BF16)
    cw_k = _cmp_weights(cmp_pe_k, cmp_w1_k, cmp_w2_k)
    cw_v = _cmp_weights(cmp_pe_v, cmp_w1_v, cmp_w2_v)

    h = ffn_ln(x_prompt.reshape(m, d), f1_in, f1_out, ln1_g, ln1_b, tm=1024)
    (qt, kct, vct, kst, vst, kwt, vwt, kcb, vcb, ksb, kwb, vstb, vwtb, gtt, uv,
     sgab) = mixer_proj_t(h, *proj_w, nb=nb, tm=512)
    n_cmp = t // CMP_STRIDE
    kcc = _compress_rows(kcb.reshape(nb, n_cmp, SUB_W), cw_k)
    vcc = _compress_rows(vcb.reshape(nb, n_cmp, SUB_W), cw_v)
    imp_t = _importance_matrix(n_cmp, t // SEL_BLOCK).T
    o_a = nsa_prompt(qt, gtt, kcc, jnp.swapaxes(vcc, 1, 2), ksb.reshape(nb, t, SLAB), vstb,
                     kwb.reshape(nb, t, SLAB), vwtb, imp_t)
    sb_tile = jnp.repeat(spatial_b.T, GROUP_CH, axis=1)
    h2 = merge_ln(h, o_a.reshape(m, Q_EXP), uv, sgab, wa, wb, wo, ln2_g, ln2_b, gmlp_ln_g,
                  gmlp_ln_b, spatial_w, sb_tile, tm=256, chunked=True)
    y_prompt = ffn_ln(h2, f2_in, f2_out, ln3_g, ln3_b, tm=1024).reshape(nb, t, d)
    wkeep = min(WINDOW, t)
    p_outs = tuple(_position_major(a) for a in
                   (kct, vct, kst, vst, kwt[:, :, t - wkeep:], vwt[:, :, t - wkeep:]))

    hs = ffn_ln(x_sample.reshape(db, d), f1_in, f1_out, ln1_g, ln1_b, tm=db)
    (qs, kcs, vcs, kss, vss, kws, vws, _, gts, uvs, sgabs) = mixer_proj(hs, *proj_w, tm=db)
    t_pad = -(-(past + n_new) // SEL_BLOCK) * SEL_BLOCK
    n_tail = (t_pad - past) // CMP_STRIDE
    tail_rows = 16

    def tail_of(new):
        flat = jnp.pad(new, ((0, 0), (0, tail_rows * SUB_W - SLAB)))
        return flat.reshape(db, tail_rows, SUB_W)

    assert n_tail <= tail_rows
    pps = min(64, n_pages)
    pool3 = lambda p: p.reshape(n_phys, PAGE_SIZE // CMP_STRIDE, SUB_W)
    kcc_s = _compress_paged(pool3(cache_cmp_k), page_table, tail_of(kcs), cw_k, pages_per_step=pps)
    vcc_s = _compress_paged(pool3(cache_cmp_v), page_table, tail_of(vcs), cw_v, pages_per_step=pps)
    n_sel_s = t_pad // SEL_BLOCK
    n_cand = -(-n_sel_s // 128) * 128
    imp_s = _importance_matrix(kcc_s.shape[1], n_cand)
    q8 = qs.reshape(db, N_HEADS, SLAB)
    o_cmp_s, idx_s = nsa_sample_cmp(q8, kcc_s, vcc_s, imp_s, qpos=past, n_sel=n_sel_s)
    idx_flat = idx_s[:, :N_KV, :SEL_TOP].reshape(db, N_KV * SEL_TOP)
    gt8 = jnp.pad(gts[:, :GATE_W].reshape(db, N_HEADS, 3), ((0, 0), (0, 0), (0, 125)))
    row3 = lambda a: a.reshape(db, 1, SLAB)
    pool_rows = lambda p: p.reshape(n_phys, PAGE_SIZE, SLAB)
    win3 = lambda a: a.reshape(db, -1, SLAB)
    o_as = nsa_sample_sel(page_table, idx_flat, pool_rows(cache_sel_k), pool_rows(cache_sel_v),
                          q8, gt8, o_cmp_s, row3(kss), row3(vss), win3(cache_win_k),
                          win3(cache_win_v), row3(kws), row3(vws), qpos=past)
    sw_row = jnp.repeat(spatial_w[:, 0, 0], GROUP_CH).reshape(1, D_B)
    sb_row = jnp.repeat(spatial_b[:, 0], GROUP_CH).reshape(1, D_B)
    hs2, vn_s = merge_ln(hs, o_as.reshape(db, Q_EXP), uvs, sgabs, wa, wb, wo, ln2_g, ln2_b,
                         gmlp_ln_g, gmlp_ln_b, sw_row, sb_row, tm=db, chunked=False)
    y_sample = ffn_ln(hs2, f2_in, f2_out, ln3_g, ln3_b, tm=db).reshape(db, n_new, d)
    kv4s = lambda a: a.reshape(db, n_new, N_KV, HEAD_DIM)
    wbuf = cache_win_k.shape[1]
    s_win_k = jnp.concatenate([cache_win_k, kv4s(kws)], axis=1)[:, -wbuf:]
    s_win_v = jnp.concatenate([cache_win_v, kv4s(vws)], axis=1)[:, -wbuf:]

    return (y_prompt, y_sample) + p_outs + (kv4s(kcs), kv4s(vcs), kv4s(kss), kv4s(vss),
                                            s_win_k, s_win_v, vn_s.reshape(db, n_new, D_B))
```

```python
import functools

import numpy as np
import jax
import jax.numpy as jnp
from jax import lax
from jax.experimental import pallas as pl
from jax.experimental.pallas import tpu as pltpu

F32 = jnp.float32
BF16 = jnp.bfloat16

D_MODEL = 1024
N_HEADS = 8
HEAD_DIM = 64
N_KV = 2
GROUP = N_HEADS // N_KV
PAGE_SIZE = 128
CMP_LEN = 32
CMP_STRIDE = 16
CMP_HIDDEN = 128
SEL_BLOCK = 64
CMP_RATIO = SEL_BLOCK // CMP_STRIDE
SEL_TOP = 16
WINDOW = 512
Q_BLOCK = 128
FORCE_SCORE = 1e4
D_B = 512
N_GROUPS_B = 4
GROUP_CH = D_B // N_GROUPS_B
CHUNK = 128
D_FF = 2816
DEPTH = 1
ALPHA = (2.0 * DEPTH) ** 0.25
LN_EPS = 1e-5
NEG_INF = -1e30
REMOVED = -3e38
Q_W = N_HEADS * HEAD_DIM
KV_W = N_KV * HEAD_DIM
GATE_W = N_HEADS * 3
SLAB = N_KV * HEAD_DIM
Q_EXP = N_HEADS * SLAB
SUB_W = CMP_STRIDE * SLAB
VMEM_LIMIT = 40 * 1024 * 1024
SEL_TILE = 1024


def _params(sem, vmem=VMEM_LIMIT, flags=None):
    return pltpu.CompilerParams(dimension_semantics=sem, vmem_limit_bytes=vmem, flags=flags)


def _layer_norm(x, g, b):
    mu = jnp.mean(x, axis=-1, keepdims=True)
    xc = x - mu
    var = jnp.mean(xc * xc, axis=-1, keepdims=True)
    return xc * lax.rsqrt(var + LN_EPS) * g + b


def _dot(a, b):
    return jnp.dot(a, b, preferred_element_type=F32)


def _dot_nt(a, b):
    return lax.dot_general(a, b, (((1,), (1,)), ((), ())), preferred_element_type=F32)


def _split3(x):
    hi = x.astype(BF16)
    r1 = x - hi.astype(F32)
    mid = r1.astype(BF16)
    lo = (r1 - mid.astype(F32)).astype(BF16)
    return hi, mid, lo


def _dot_exact_rhs(x, a_bf16):
    hi, mid, lo = _split3(x)
    return _dot(hi, a_bf16) + _dot(mid, a_bf16) + _dot(lo, a_bf16)


def _ffn_ln_kernel(x_ref, wg_ref, wu_ref, wo_ref, g_ref, b_ref, o_ref, xb_ref, acc_ref, *, n_f):
    f = pl.program_id(1)

    @pl.when(f == 0)
    def _():
        xb_ref[...] = x_ref[...].astype(BF16)
        acc_ref[...] = jnp.zeros_like(acc_ref)

    xb = xb_ref[...]
    gate = _dot(xb, wg_ref[...])
    up = _dot(xb, wu_ref[...])
    hid = (gate * jax.nn.sigmoid(gate)) * up
    acc_ref[...] += _dot(hid.astype(BF16), wo_ref[...])

    @pl.when(f == n_f - 1)
    def _():
        y = ALPHA * x_ref[...] + 0.5 * acc_ref[...]
        o_ref[...] = _layer_norm(y, g_ref[...], b_ref[...])


def ffn_ln(x, w_in_b, w_out_b, g, b, *, tm, tf=256):
    m, d = x.shape
    d_ff = w_out_b.shape[0]
    n_f = d_ff // tf
    return pl.pallas_call(
        functools.partial(_ffn_ln_kernel, n_f=n_f),
        grid=(m // tm, n_f),
        in_specs=[
            pl.BlockSpec((tm, d), lambda i, f: (i, 0)),
            pl.BlockSpec((d, tf), lambda i, f: (0, f)),
            pl.BlockSpec((d, tf), lambda i, f: (0, f + n_f)),
            pl.BlockSpec((tf, d), lambda i, f: (f, 0)),
            pl.BlockSpec((1, d), lambda i, f: (0, 0)),
            pl.BlockSpec((1, d), lambda i, f: (0, 0)),
        ],
        out_specs=pl.BlockSpec((tm, d), lambda i, f: (i, 0)),
        out_shape=jax.ShapeDtypeStruct((m, d), F32),
        scratch_shapes=[pltpu.VMEM((tm, d), BF16), pltpu.VMEM((tm, d), F32)],
        compiler_params=_params(("parallel", "arbitrary")),
        name="ffn_ln",
    )(x, w_in_b, w_in_b, w_out_b, g.reshape(1, d), b.reshape(1, d))


def _mixer_proj_kernel(h_ref, wq_ref, wkv_ref, wgt_ref, wuv_ref, wgab_ref,
                       q_ref, kc_ref, vc_ref, ks_ref, vs_ref, kw_ref, vw_ref,
                       kvb_ref, gt_ref, uv_ref, sgab_ref):
    hb = h_ref[...].astype(BF16)
    q_ref[...] = (_dot(hb, wq_ref[...]) * (HEAD_DIM ** -0.5)).astype(BF16)
    kv = _dot(hb, wkv_ref[...])
    for j, ref in enumerate((kc_ref, vc_ref, ks_ref, vs_ref, kw_ref, vw_ref)):
        ref[...] = kv[:, j * SLAB:(j + 1) * SLAB]
    kvb_ref[...] = kv.astype(BF16)
    gt_ref[...] = jax.nn.sigmoid(_dot(hb, wgt_ref[...]))
    uv_ref[...] = jax.nn.gelu(_dot(hb, wuv_ref[...]))
    sgab_ref[...] = jax.nn.sigmoid(_dot(hb, wgab_ref[...])).astype(BF16)


def mixer_proj(h, wq, wkv, wgt, wuv, wgab, *, tm):
    m, d = h.shape
    row = lambda n: pl.BlockSpec((tm, n), lambda i: (i, 0))
    full = lambda w: pl.BlockSpec(w.shape, lambda i: (0, 0))
    out_shape = (
        [jax.ShapeDtypeStruct((m, Q_EXP), BF16)]
        + [jax.ShapeDtypeStruct((m, SLAB), F32)] * 6
        + [jax.ShapeDtypeStruct((m, 6 * SLAB), BF16),
           jax.ShapeDtypeStruct((m, 128), F32),
           jax.ShapeDtypeStruct((m, 2 * D_B), F32),
           jax.ShapeDtypeStruct((m, 2 * D_MODEL), BF16)])
    out_specs = ([row(Q_EXP)] + [row(SLAB)] * 6
                 + [row(6 * SLAB), row(128), row(2 * D_B), row(2 * D_MODEL)])
    return pl.pallas_call(
        _mixer_proj_kernel,
        grid=(m // tm,),
        in_specs=[row(d), full(wq), full(wkv), full(wgt), full(wuv), full(wgab)],
        out_specs=out_specs,
        out_shape=out_shape,
        compiler_params=_params(("parallel",)),
        name="mixer_proj",
    )(h, wq, wkv, wgt, wuv, wgab)


def _mixer_proj_t_kernel(h_ref, wq_ref, wkv_ref, wgt_ref, wuv_ref, wgab_ref,
                         qt_ref, kct_ref, vct_ref, kst_ref, vst_ref, kwt_ref, vwt_ref,
                         kcb_ref, vcb_ref, ksb_ref, kwb_ref, vstb_ref, vwtb_ref,
                         gtt_ref, uv_ref, sgab_ref):
    hb = h_ref[...].astype(BF16)
    q = _dot(hb, wq_ref[...]) * (HEAD_DIM ** -0.5)
    for h in range(N_HEADS):
        qt_ref[h * SLAB:(h + 1) * SLAB, :] = q[:, h * SLAB:(h + 1) * SLAB].T.astype(BF16)
    kv = _dot(hb, wkv_ref[...])
    slabs = [kv[:, j * SLAB:(j + 1) * SLAB] for j in range(6)]
    for slab, ref in zip(slabs, (kct_ref, vct_ref, kst_ref, vst_ref, kwt_ref, vwt_ref)):
        ref[...] = slab.T
    kcb_ref[...] = slabs[0].astype(BF16)
    vcb_ref[...] = slabs[1].astype(BF16)
    ksb_ref[...] = slabs[2].astype(BF16)
    kwb_ref[...] = slabs[4].astype(BF16)
    vstb_ref[...] = slabs[3].T.astype(BF16)
    vwtb_ref[...] = slabs[5].T.astype(BF16)
    gtt_ref[...] = jax.nn.sigmoid(_dot(hb, wgt_ref[...])).T
    uv_ref[...] = jax.nn.gelu(_dot(hb, wuv_ref[...]))
    sgab_ref[...] = jax.nn.sigmoid(_dot(hb, wgab_ref[...])).astype(BF16)


def mixer_proj_t(h, wq, wkv, wgt, wuv, wgab, *, nb, tm):
    m, d = h.shape
    t = m // nb
    per_b = t // tm
    row = lambda n: pl.BlockSpec((tm, n), lambda i: (i, 0))
    col = lambda n: pl.BlockSpec((None, n, tm), lambda i: (i // per_b, 0, i % per_b))
    full = lambda w: pl.BlockSpec(w.shape, lambda i: (0, 0))
    sds = jax.ShapeDtypeStruct
    out_shape = ([sds((nb, Q_EXP, t), BF16)] + [sds((nb, SLAB, t), F32)] * 6
                 + [sds((m, SLAB), BF16)] * 4 + [sds((nb, SLAB, t), BF16)] * 2
                 + [sds((nb, 128, t), F32), sds((m, 2 * D_B), F32), sds((m, 2 * D_MODEL), BF16)])
    out_specs = ([col(Q_EXP)] + [col(SLAB)] * 6 + [row(SLAB)] * 4 + [col(SLAB)] * 2
                 + [col(128), row(2 * D_B), row(2 * D_MODEL)])
    return pl.pallas_call(
        _mixer_proj_t_kernel,
        grid=(m // tm,),
        in_specs=[row(d), full(wq), full(wkv), full(wgt), full(wuv), full(wgab)],
        out_specs=out_specs,
        out_shape=out_shape,
        compiler_params=_params(("parallel",)),
        name="mixer_proj_t",
    )(h, wq, wkv, wgt, wuv, wgab)


def _cmp_ab_kernel(pt_ref, *refs, n_pages):
    del pt_ref
    page_refs = refs[:n_pages]
    w_ref = refs[n_pages]
    o_ref = refs[n_pages + 1]
    if n_pages == 1:
        x = page_refs[0][...]
    else:
        x = jnp.concatenate([r[...] for r in page_refs], axis=0)
    o_ref[...] = _dot(x.astype(BF16), w_ref[...])


def cmp_ab(pool, page_table, w_ab, *, pages_per_step):
    nb, n_pages = page_table.shape
    rows = pool.shape[1]
    pp = pages_per_step
    n_steps = n_pages // pp

    def page_spec(k):
        return pl.BlockSpec((None, rows, SUB_W), lambda b, s, pt: (pt[b, s * pp + k], 0, 0))

    grid_spec = pltpu.PrefetchScalarGridSpec(
        num_scalar_prefetch=1,
        grid=(nb, n_steps),
        in_specs=[page_spec(k) for k in range(pp)]
        + [pl.BlockSpec(w_ab.shape, lambda b, s, pt: (0, 0))],
        out_specs=pl.BlockSpec((None, pp * rows, w_ab.shape[1]), lambda b, s, pt: (b, s, 0)),
    )
    return pl.pallas_call(
        functools.partial(_cmp_ab_kernel, n_pages=pp),
        grid_spec=grid_spec,
        out_shape=jax.ShapeDtypeStruct((nb, n_pages * rows, w_ab.shape[1]), F32),
        compiler_params=_params(("parallel", "arbitrary")),
        name="cmp_ab",
    )(page_table, *([pool] * pp), w_ab)


def _cmp_ab_paged_kernel(pt_ref, *refs, n_pages):
    del pt_ref
    page_refs = refs[:n_pages]
    w_ref, o_ref, rows_ref = refs[n_pages:]
    for k, r in enumerate(page_refs):
        rows_ref[k * PAGE_SIZE:(k + 1) * PAGE_SIZE, :] = r[...]
    n_sub = n_pages * PAGE_SIZE // CMP_STRIDE
    x = jnp.concatenate([rows_ref[pl.ds(p, n_sub, stride=CMP_STRIDE), :].astype(BF16)
                         for p in range(CMP_STRIDE)], axis=1)
    o_ref[...] = _dot(x, w_ref[...])


def cmp_ab_paged(pool, page_table, w_ab, *, pages_per_step):
    nb, n_pages = page_table.shape
    pp = pages_per_step
    sub_per_page = PAGE_SIZE // CMP_STRIDE

    def page_spec(k):
        return pl.BlockSpec((None, PAGE_SIZE, SLAB), lambda b, s, pt: (pt[b, s * pp + k], 0, 0))

    grid_spec = pltpu.PrefetchScalarGridSpec(
        num_scalar_prefetch=1,
        grid=(nb, n_pages // pp),
        in_specs=[page_spec(k) for k in range(pp)]
        + [pl.BlockSpec(w_ab.shape, lambda b, s, pt: (0, 0))],
        out_specs=pl.BlockSpec((None, pp * sub_per_page, w_ab.shape[1]), lambda b, s, pt: (b, s, 0)),
        scratch_shapes=[pltpu.VMEM((pp * PAGE_SIZE, SLAB), F32)],
    )
    return pl.pallas_call(
        functools.partial(_cmp_ab_paged_kernel, n_pages=pp),
        grid_spec=grid_spec,
        out_shape=jax.ShapeDtypeStruct((nb, n_pages * sub_per_page, w_ab.shape[1]), F32),
        compiler_params=_params(("parallel", "arbitrary")),
        name="cmp_ab_paged",
    )(page_table, *([pool] * pp), w_ab)


def _cmp_fin_kernel(*refs, has_tail):
    if has_tail:
        ab_ref, tail_ref, pe_ref, w1_ref, w2_ref, o_ref = refs
    else:
        ab_ref, pe_ref, w1_ref, w2_ref, o_ref = refs
    hw = 2 * CMP_HIDDEN
    n = ab_ref.shape[0]
    pe_h = _dot(pe_ref[...], w1_ref[...])[0:1, :]
    bias = jnp.concatenate([pe_h, pe_h], axis=1)
    w2 = w2_ref[...]
    first = ab_ref[:, 0:hw]
    second = pltpu.roll(ab_ref[:, hw:2 * hw], n - 1, 0)
    if has_tail:
        t_first = tail_ref[:, 0:hw]
        t_second = tail_ref[:, hw:2 * hw]
        nt = tail_ref.shape[0]
        is_last = lax.broadcasted_iota(jnp.int32, (n, 1), 0) == n - 1
        second = jnp.where(is_last, t_second[0:1, :], second)
        t_hid = t_first + pltpu.roll(t_second, nt - 1, 0) + bias
        o_ref[n:n + nt, :] = _dot(jax.nn.gelu(t_hid).astype(BF16), w2).astype(BF16)
        n_out = o_ref.shape[0]
        if n_out > n + nt:
            o_ref[n + nt:n_out, :] = jnp.zeros((n_out - n - nt, SLAB), BF16)
    hid = first + second + bias
    o_ref[0:n, :] = _dot(jax.nn.gelu(hid).astype(BF16), w2).astype(BF16)


def cmp_fin(ab, tail, pe8, w1b, w2x):
    nb, n, w = ab.shape
    has_tail = tail is not None
    nt = tail.shape[1] if has_tail else 0
    full = lambda a: pl.BlockSpec(a.shape, lambda b: (0, 0))
    in_specs = [pl.BlockSpec((None, n, w), lambda b: (b, 0, 0))]
    args = [ab]
    if has_tail:
        in_specs.append(pl.BlockSpec((None, nt, w), lambda b: (b, 0, 0)))
        args.append(tail)
    in_specs += [full(pe8), full(w1b), full(w2x)]
    args += [pe8, w1b, w2x]
    n_out = -(-(n + nt) // 128) * 128 if has_tail else n
    return pl.pallas_call(
        functools.partial(_cmp_fin_kernel, has_tail=has_tail),
        grid=(nb,),
        in_specs=in_specs,
        out_specs=pl.BlockSpec((None, n_out, SLAB), lambda b: (b, 0, 0)),
        out_shape=jax.ShapeDtypeStruct((nb, n_out, SLAB), BF16),
        compiler_params=_params(("parallel",)),
        name="cmp_fin",
    )(*args)


def _slope_of_head(h):
    return 2.0 ** (-(h + 1))


def _slope_column():
    head = lax.broadcasted_iota(jnp.int32, (N_HEADS, 1), 0)
    slope = jnp.zeros((N_HEADS, 1), F32)
    for h in range(N_HEADS):
        slope = jnp.where(head == h, _slope_of_head(h), slope)
    return slope


def _top_k_mask_t(score_t, k):
    n = score_t.shape[0]
    cand_idx = lax.broadcasted_iota(jnp.int32, score_t.shape, 0).astype(F32)

    def body(_, carry):
        s, sel = carry
        m = jnp.max(s, axis=0, keepdims=True)
        first = jnp.min(jnp.where(s == m, cand_idx, float(n)), axis=0, keepdims=True)
        pick = cand_idx == first
        return jnp.where(pick, REMOVED, s), jnp.where(pick, 1.0, sel)

    _, sel = lax.fori_loop(0, k, body, (score_t, jnp.zeros_like(score_t)), unroll=True)
    return sel


def _softmax_masked(s, mask):
    s = jnp.where(mask, s, NEG_INF)
    m = jnp.max(s, axis=-1, keepdims=True)
    e = jnp.exp(s - m)
    return jnp.where(mask, e / jnp.sum(e, axis=-1, keepdims=True), 0.0)


def _softmax_masked_t(s, mask):
    s = jnp.where(mask, s, NEG_INF)
    m = jnp.max(s, axis=0, keepdims=True)
    e = jnp.exp(s - m)
    return jnp.where(mask, e * (1.0 / jnp.sum(e, axis=0, keepdims=True)), 0.0)


def _nsa_prompt_kernel(qt_ref, gtt_ref, kc_ref, vct_ref, ks_ref, vst_ref, kw_ref, vwt_ref, impt_ref,
                       o_ref, selt_ref, bias_c_ref, bias_w_ref, bias_s_ref):
    qb = pl.program_id(1)
    s0 = qb * Q_BLOCK
    nq = Q_BLOCK
    hq = N_HEADS * nq
    nc = kc_ref.shape[0]
    n_sel = ks_ref.shape[0] // SEL_BLOCK
    tk = SEL_TILE
    assert HEAD_DIM == 64 and SEL_BLOCK == 64 and Q_BLOCK == 128

    lane = lax.broadcasted_iota(jnp.int32, (1, hq), 1)
    q_loc = lane & (nq - 1)
    qpos_f = (s0 + q_loc).astype(F32)
    qpos_q = s0 + lax.broadcasted_iota(jnp.int32, (1, nq), 1)
    row_half = lax.broadcasted_iota(jnp.int32, (SLAB, 1), 0) >> 6
    slope = jnp.zeros((1, hq), F32)
    for h in range(N_HEADS):
        slope = jnp.where((lane >> 7) == h, _slope_of_head(h), slope)
    n_win = WINDOW + nq
    k_loc = lax.broadcasted_iota(jnp.int32, (tk, 1), 0)

    @pl.when(qb == 0)
    def _():
        q_loc_f = q_loc.astype(F32)
        c_end = lax.broadcasted_iota(jnp.int32, (nc, 1), 0) * CMP_STRIDE + (CMP_LEN - 1)
        bias_c_ref[...] = slope * (q_loc_f - c_end.astype(F32))
        w_loc = lax.broadcasted_iota(jnp.int32, (n_win, 1), 0)
        bias_w_ref[...] = slope * (q_loc_f - w_loc.astype(F32))
        bias_s_ref[...] = slope * k_loc.astype(F32)

    j_idx = lax.broadcasted_iota(jnp.int32, (n_sel, 1), 0)
    cur = qpos_q >> 6
    forced = (j_idx == 0) | (j_idx == cur) | (j_idx == cur - 1)
    valid = (j_idx * SEL_BLOCK) <= qpos_q

    w0 = pl.multiple_of(jnp.maximum(s0 - WINDOW, 0), Q_BLOCK)
    kw = kw_ref[pl.ds(w0, n_win), :]
    vwt = vwt_ref[:, pl.ds(w0, n_win)]

    assert tk % nq == 0
    n_full = s0 // tk
    blocks_per_tile = tk // SEL_BLOCK

    qt = jnp.concatenate([qt_ref[h * SLAB:(h + 1) * SLAB, :] for h in range(N_HEADS)],
                         axis=1)

    bias_c = bias_c_ref[...]
    mask_c = bias_c >= slope * (-s0).astype(F32)
    p = _softmax_masked_t(_dot(kc_ref[...], qt) - bias_c, mask_c)
    o_cmp = _dot(vct_ref[...], p.astype(BF16))

    imp_t = impt_ref[...]
    scores = []
    for g in range(N_KV):
        p_sum = p[:, g * GROUP * nq:(g * GROUP + 1) * nq]
        for r in range(1, GROUP):
            p_sum = p_sum + p[:, (g * GROUP + r) * nq:(g * GROUP + r + 1) * nq]
        hi, mid, lo = _split3(p_sum)
        p_slc = _dot(imp_t, hi) + _dot(imp_t, mid) + _dot(imp_t, lo)
        scores.append(jnp.where(valid, jnp.where(forced, FORCE_SCORE, p_slc), NEG_INF))
    sel = _top_k_mask_t(jnp.concatenate(scores, axis=1), min(SEL_TOP, n_sel))
    for g in range(N_KV):
        selt_ref[g] = sel[:, g * nq:(g + 1) * nq]


    def sel_tile(t, carry, diagonal):
        m_old, l_old, acc_old = carry
        kt = pl.multiple_of(t * tk, tk)
        k = ks_ref[pl.ds(kt, tk), :]
        vt = vst_ref[:, pl.ds(kt, tk)]
        s_rel = _dot(k, qt) + bias_s_ref[...]
        shift = slope * (kt.astype(F32) - qpos_f)
        cols = []
        for g in range(N_KV):
            picked = jnp.concatenate(
                [jnp.broadcast_to(selt_ref[g, pl.ds(t * blocks_per_tile + jj, 1), :],
                                  (SEL_BLOCK, nq))
                 for jj in range(blocks_per_tile)], axis=0) > 0.5
            if diagonal:
                picked = picked & ((kt + k_loc) <= qpos_q)
            for r in range(GROUP):
                h = g * GROUP + r
                cols.append(jnp.where(picked, s_rel[:, h * nq:(h + 1) * nq], NEG_INF))
        s_rel = jnp.concatenate(cols, axis=1)
        m_new = jnp.maximum(m_old, jnp.max(s_rel, axis=0, keepdims=True) + shift)
        alpha = jnp.exp(m_old - m_new)
        p_t = jnp.exp(s_rel - (m_new - shift))
        l_new = alpha * l_old + jnp.sum(p_t, axis=0, keepdims=True)
        acc_new = alpha * acc_old + _dot(vt, p_t.astype(BF16))
        return m_new, l_new, acc_new

    init = (jnp.full((1, hq), NEG_INF, F32), jnp.zeros((1, hq), F32), jnp.zeros((SLAB, hq), F32))
    state = lax.fori_loop(0, n_full, functools.partial(sel_tile, diagonal=False), init)
    _, l_sel, acc_sel = sel_tile(n_full, state, diagonal=True)
    o_sel = acc_sel * (1.0 / l_sel)

    bias_w = bias_w_ref[...]
    lead = (s0 - w0).astype(F32)
    mask_w = (bias_w >= slope * (-lead)) & (bias_w < slope * (WINDOW - lead))
    o_win = _dot(vwt, _softmax_masked_t(_dot(kw, qt) - bias_w, mask_w).astype(BF16))

    for h in range(N_HEADS):
        cols = slice(h * nq, (h + 1) * nq)
        o = (gtt_ref[3 * h:3 * h + 1, :] * o_cmp[:, cols]
             + gtt_ref[3 * h + 1:3 * h + 2, :] * o_sel[:, cols]
             + gtt_ref[3 * h + 2:3 * h + 3, :] * o_win[:, cols])
        o_ref[:, h * SLAB:(h + 1) * SLAB] = jnp.where(row_half == h // GROUP, o, 0.0).T.astype(BF16)


def nsa_prompt(qt, gtt, kc, vct, ks, vst, kw, vwt, imp_t):
    nb, _, t = qt.shape
    nc = kc.shape[1]
    rows = pl.BlockSpec((None, t, SLAB), lambda b, i: (b, 0, 0))
    cols = pl.BlockSpec((None, SLAB, t), lambda b, i: (b, 0, 0))
    return pl.pallas_call(
        _nsa_prompt_kernel,
        grid=(nb, t // Q_BLOCK),
        in_specs=[
            pl.BlockSpec((None, Q_EXP, Q_BLOCK), lambda b, i: (b, 0, i)),
            pl.BlockSpec((None, 128, Q_BLOCK), lambda b, i: (b, 0, i)),
            pl.BlockSpec((None, nc, SLAB), lambda b, i: (b, 0, 0)),
            pl.BlockSpec((None, SLAB, nc), lambda b, i: (b, 0, 0)),
            rows, cols, rows, cols,
            pl.BlockSpec(imp_t.shape, lambda b, i: (0, 0)),
        ],
        out_specs=pl.BlockSpec((None, Q_BLOCK, Q_EXP), lambda b, i: (b, i, 0)),
        out_shape=jax.ShapeDtypeStruct((nb, t, Q_EXP), BF16),
        scratch_shapes=[pltpu.VMEM((N_KV, t // SEL_BLOCK, Q_BLOCK), F32),
                        pltpu.VMEM((nc, N_HEADS * Q_BLOCK), F32),
                        pltpu.VMEM((WINDOW + Q_BLOCK, N_HEADS * Q_BLOCK), F32),
                        pltpu.VMEM((SEL_TILE, N_HEADS * Q_BLOCK), F32)],
        compiler_params=_params(("parallel", "arbitrary")),
        name="nsa_prompt",
    )(qt, gtt, kc, vct, ks, vst, kw, vwt, imp_t)


def _nsa_sample_cmp_kernel(q_ref, kc_ref, vc_ref, imp_ref, oc_ref, idx_ref, *, qpos, n_sel):
    nc = kc_ref.shape[0]
    n_cand = imp_ref.shape[1]
    q = q_ref[...]
    head = lax.broadcasted_iota(jnp.int32, (N_HEADS, 1), 0)
    slope = _slope_column()
    c_end = lax.broadcasted_iota(jnp.int32, (1, nc), 1) * CMP_STRIDE + (CMP_LEN - 1)
    dist = qpos - c_end
    mask = dist >= 0
    s = _dot_nt(q, kc_ref[...]) - slope * dist.astype(F32)
    p = _softmax_masked(s, mask)
    oc_ref[...] = _dot(p.astype(BF16), vc_ref[...])

    p_sum = jnp.zeros_like(p)
    for g in range(N_KV):
        grp = jnp.sum(p[g * GROUP:(g + 1) * GROUP], axis=0, keepdims=True)
        p_sum = jnp.where(head == g, grp, p_sum)
    p_slc = _dot_exact_rhs(p_sum, imp_ref[...])

    j_idx = lax.broadcasted_iota(jnp.int32, (1, n_cand), 1)
    cur = qpos // SEL_BLOCK
    forced = (j_idx == 0) | (j_idx == cur) | (j_idx == cur - 1)
    valid = (j_idx * SEL_BLOCK) <= qpos
    score = jnp.where(valid, jnp.where(forced, FORCE_SCORE, p_slc), NEG_INF)
    score = jnp.where(j_idx < n_sel, score, REMOVED)

    cand = lax.broadcasted_iota(jnp.int32, score.shape, 1).astype(F32)
    out_lane = lax.broadcasted_iota(jnp.int32, (N_HEADS, 128), 1)

    def body(k, carry):
        sc, out = carry
        m = jnp.max(sc, axis=1, keepdims=True)
        first = jnp.min(jnp.where(sc == m, cand, float(n_cand)), axis=1, keepdims=True)
        return (jnp.where(cand == first, REMOVED, sc),
                jnp.where(out_lane == k, first.astype(jnp.int32), out))

    _, out = lax.fori_loop(0, SEL_TOP, body, (score, jnp.zeros((N_HEADS, 128), jnp.int32)),
                           unroll=True)
    idx_ref[...] = out


def nsa_sample_cmp(q8, kc, vc, imp, *, qpos, n_sel):
    nb = q8.shape[0]
    nc = kc.shape[1]
    return pl.pallas_call(
        functools.partial(_nsa_sample_cmp_kernel, qpos=qpos, n_sel=n_sel),
        grid=(nb,),
        in_specs=[
            pl.BlockSpec((None, N_HEADS, SLAB), lambda b: (b, 0, 0)),
            pl.BlockSpec((None, nc, SLAB), lambda b: (b, 0, 0)),
            pl.BlockSpec((None, nc, SLAB), lambda b: (b, 0, 0)),
            pl.BlockSpec(imp.shape, lambda b: (0, 0)),
        ],
        out_specs=[pl.BlockSpec((None, N_HEADS, SLAB), lambda b: (b, 0, 0)),
                   pl.BlockSpec((None, N_HEADS, 128), lambda b: (b, 0, 0))],
        out_shape=[jax.ShapeDtypeStruct((nb, N_HEADS, SLAB), F32),
                   jax.ShapeDtypeStruct((nb, N_HEADS, 128), jnp.int32)],
        compiler_params=_params(("parallel",)),
        name="nsa_sample_cmp",
    )(q8, kc, vc, imp)


def _nsa_sample_sel_kernel(pt_ref, idx_ref, *refs, qpos, n_past_blocks):
    del pt_ref
    n_pg = N_KV * SEL_TOP
    k_pages = refs[:n_pg]
    v_pages = refs[n_pg:2 * n_pg]
    (q_ref, gt_ref, oc_ref, kn_ref, vn_ref, wk_ref, wv_ref, wkn_ref, wvn_ref,
     o_ref) = refs[2 * n_pg:]
    b = pl.program_id(0)
    q = q_ref[...]
    head = lax.broadcasted_iota(jnp.int32, (N_HEADS, 1), 0)
    slope = _slope_column()
    grp_of_row = head >> 2
    assert GROUP == 4 and HEAD_DIM == 64 and SEL_BLOCK == 64
    lane_half = lax.broadcasted_iota(jnp.int32, (1, SLAB), 1) >> 6
    blocks_per_page = PAGE_SIZE // SEL_BLOCK
    n_keys = SEL_TOP * SEL_BLOCK
    key_slot = lax.broadcasted_iota(jnp.int32, (1, n_keys), 1) >> 6
    key_off = lax.broadcasted_iota(jnp.int32, (1, n_keys), 1) & (SEL_BLOCK - 1)
    qf = q.astype(F32)

    s_new = jnp.sum(qf * kn_ref[...], axis=1, keepdims=True)

    o_sel = jnp.zeros((N_HEADS, SLAB), F32)
    for g in range(N_KV):
        ks, vs = [], []
        start = jnp.zeros((1, n_keys), jnp.int32)
        for k in range(SEL_TOP):
            j = idx_ref[b, g * SEL_TOP + k]
            off = pl.multiple_of((j % blocks_per_page) * SEL_BLOCK, SEL_BLOCK)
            ks.append(k_pages[g * SEL_TOP + k][pl.ds(off, SEL_BLOCK), :])
            vs.append(v_pages[g * SEL_TOP + k][pl.ds(off, SEL_BLOCK), :])
            start = jnp.where(key_slot == k, j * SEL_BLOCK, start)
        k_all = jnp.concatenate(ks, axis=0).astype(BF16)
        v_all = jnp.concatenate(vs, axis=0).astype(BF16)
        spos = start + key_off
        dist = qpos - spos
        mask = (dist >= 0) & (start < n_past_blocks * SEL_BLOCK)
        s = _dot_nt(q, k_all) - slope * dist.astype(F32)
        s = jnp.where(mask, s, NEG_INF)
        m = jnp.maximum(jnp.max(s, axis=-1, keepdims=True), s_new)
        e = jnp.exp(s - m)
        e_new = jnp.exp(s_new - m)
        l = jnp.sum(e, axis=-1, keepdims=True) + e_new
        o_g = (_dot(e.astype(BF16), v_all) + e_new * vn_ref[...]) / l
        o_sel = jnp.where(grp_of_row == g, o_g, o_sel)

    n_win = wk_ref.shape[0]
    dist_w = n_win - lax.broadcasted_iota(jnp.int32, (1, n_win), 1)
    mask_w = (dist_w >= 0) & (dist_w < WINDOW)
    s_w = _dot_nt(q, wk_ref[...].astype(BF16)) - slope * dist_w.astype(F32)
    s_w = jnp.where(mask_w, s_w, NEG_INF)
    s_wn = jnp.sum(qf * wkn_ref[...], axis=1, keepdims=True)
    m_w = jnp.maximum(jnp.max(s_w, axis=-1, keepdims=True), s_wn)
    e_w = jnp.exp(s_w - m_w)
    e_wn = jnp.exp(s_wn - m_w)
    l_w = jnp.sum(e_w, axis=-1, keepdims=True) + e_wn
    o_win = (_dot(e_w.astype(BF16), wv_ref[...].astype(BF16)) + e_wn * wvn_ref[...]) / l_w

    gt = gt_ref[...]
    o = gt[:, 0:1] * oc_ref[...] + gt[:, 1:2] * o_sel + gt[:, 2:3] * o_win
    o_ref[...] = jnp.where(lane_half == grp_of_row, o, 0.0).astype(BF16)


def nsa_sample_sel(page_table, idx, pool_k, pool_v, q8, gt8, o_cmp, k_new, v_new,
                   win_k, win_v, wk_new, wv_new, *, qpos):
    nb, n_pages = page_table.shape
    n_win = win_k.shape[1]
    blocks_per_page = PAGE_SIZE // SEL_BLOCK

    def page_spec(i):
        def index_map(b, pt, ix):
            page = jnp.minimum(ix[b, i] // blocks_per_page, n_pages - 1)
            return (pt[b, page], 0, 0)
        return pl.BlockSpec((None, PAGE_SIZE, SLAB), index_map)

    per_b = lambda r: pl.BlockSpec((None, r, SLAB), lambda b, pt, ix: (b, 0, 0))
    n_pg = N_KV * SEL_TOP
    grid_spec = pltpu.PrefetchScalarGridSpec(
        num_scalar_prefetch=2,
        grid=(nb,),
        in_specs=[page_spec(i) for i in range(n_pg)] * 2
        + [per_b(N_HEADS), per_b(N_HEADS), per_b(N_HEADS), per_b(1), per_b(1),
           per_b(n_win), per_b(n_win), per_b(1), per_b(1)],
        out_specs=per_b(N_HEADS),
    )
    return pl.pallas_call(
        functools.partial(_nsa_sample_sel_kernel, qpos=qpos, n_past_blocks=n_pages * blocks_per_page),
        grid_spec=grid_spec,
        out_shape=jax.ShapeDtypeStruct((nb, N_HEADS, SLAB), BF16),
        compiler_params=_params(("arbitrary",)),
        name="nsa_sample_sel",
    )(page_table, idx, *([pool_k] * n_pg), *([pool_v] * n_pg), q8, gt8, o_cmp, k_new, v_new,
      win_k, win_v, wk_new, wv_new)


def _merge_ln_kernel(*refs, chunked):
    if chunked:
        (h_ref, oa_ref, uv_ref, sg_ref, wa_ref, wb_ref, wo_ref, g2_ref, b2_ref, lg_ref, lb_ref,
         sw_ref, sb_ref, o_ref) = refs
    else:
        (h_ref, oa_ref, uv_ref, sg_ref, wa_ref, wb_ref, wo_ref, g2_ref, b2_ref, lg_ref, lb_ref,
         sw_ref, sb_ref, o_ref, vn_ref) = refs
    tm = h_ref.shape[0]
    u = uv_ref[:, 0:D_B]
    vn = _layer_norm(uv_ref[:, D_B:2 * D_B], lg_ref[...], lb_ref[...])
    if chunked:
        row = lax.broadcasted_iota(jnp.int32, (CHUNK, CHUNK), 0)
        col = lax.broadcasted_iota(jnp.int32, (CHUNK, CHUNK), 1)
        vnb = vn.astype(BF16)
        chunks = []
        for c in range(tm // CHUNK):
            parts = []
            for hg in range(N_GROUPS_B):
                w = jnp.where(col <= row, sw_ref[hg], 0.0).astype(BF16)
                parts.append(_dot(w, vnb[c * CHUNK:(c + 1) * CHUNK, hg * GROUP_CH:(hg + 1) * GROUP_CH]))
            chunks.append(jnp.concatenate(parts, axis=1) + sb_ref[...])
        s = jnp.concatenate(chunks, axis=0)
    else:
        s = vn * sw_ref[...] + sb_ref[...]
        vn_ref[...] = vn
    z = u * s
    branch_a = _dot(oa_ref[...], wa_ref[...])
    branch_b = _dot(z.astype(BF16), wb_ref[...])
    merged = (sg_ref[:, 0:D_MODEL].astype(F32) * branch_a
              + sg_ref[:, D_MODEL:2 * D_MODEL].astype(F32) * branch_b)
    y = ALPHA * h_ref[...] + _dot(merged.astype(BF16), wo_ref[...])
    o_ref[...] = _layer_norm(y, g2_ref[...], b2_ref[...])


def merge_ln(h, oa, uv, sgab, wa, wb, wo, g2, b2, lg, lb, sw, sb, *, tm, chunked):
    m, d = h.shape
    row = lambda n: pl.BlockSpec((tm, n), lambda i: (i, 0))
    full = lambda a: pl.BlockSpec(a.shape, lambda i: (0,) * a.ndim)
    vec = lambda a: a.reshape(1, -1)
    g2, b2, lg, lb = vec(g2), vec(b2), vec(lg), vec(lb)
    out_shape = [jax.ShapeDtypeStruct((m, d), F32)]
    out_specs = [row(d)]
    if not chunked:
        out_shape.append(jax.ShapeDtypeStruct((m, D_B), F32))
        out_specs.append(row(D_B))
    res = pl.pallas_call(
        functools.partial(_merge_ln_kernel, chunked=chunked),
        grid=(m // tm,),
        in_specs=[row(d), row(Q_EXP), row(2 * D_B), row(2 * D_MODEL), full(wa), full(wb), full(wo),
                  full(g2), full(b2), full(lg), full(lb), full(sw), full(sb)],
        out_specs=out_specs,
        out_shape=out_shape,
        compiler_params=_params(("parallel",)),
        name="merge_ln",
    )(h, oa, uv, sgab, wa, wb, wo, g2, b2, lg, lb, sw, sb)
    return res if not chunked else res[0]


def _group_mask():
    return (np.arange(N_HEADS)[:, None] // GROUP == np.arange(N_KV)[None, :]).astype(np.float32)


def _split_w_in(w_in):
    sizes = (Q_W,) + (KV_W,) * 6 + (GATE_W, D_B, D_B, D_MODEL, D_MODEL)
    cuts = np.cumsum(sizes)[:-1].tolist()
    q, kc, vc, ks, vs, kw, vw, g, u, v, ga, gb = jnp.split(w_in, cuts, axis=1)
    d = w_in.shape[0]
    gm = jnp.asarray(_group_mask())
    wq = (q.reshape(d, N_HEADS, 1, HEAD_DIM) * gm[None, :, :, None]).reshape(d, Q_EXP)
    wkv = jnp.concatenate([kc, vc, ks, vs, kw, vw], axis=1)
    wgt = jnp.pad(g, ((0, 0), (0, 128 - GATE_W)))
    wuv = jnp.concatenate([u, v], axis=1)
    wgab = jnp.concatenate([ga, gb], axis=1)
    return tuple(w.astype(BF16) for w in (wq, wkv, wgt, wuv, wgab))


def _expand_branch_a(w_a):
    gm = jnp.asarray(_group_mask())
    w = w_a.reshape(N_HEADS, 1, HEAD_DIM, w_a.shape[1]) * gm[:, :, None, None]
    return w.reshape(Q_EXP, w_a.shape[1]).astype(BF16)


def _cmp_weights(pe, w1, w2):
    n_sub = CMP_LEN // CMP_STRIDE
    w1r = w1.reshape(n_sub, CMP_STRIDE, HEAD_DIM, CMP_HIDDEN)
    eye = jnp.eye(N_KV, dtype=w1.dtype)
    w_ab = jnp.einsum('spdc,gh->pgdshc', w1r, eye).reshape(SUB_W, n_sub * N_KV * CMP_HIDDEN)
    w2x = jnp.einsum('cd,gh->gchd', w2, eye).reshape(N_KV * CMP_HIDDEN, SLAB)
    pe8 = jnp.pad(pe.reshape(1, -1), ((0, 7), (0, 0)))
    return w_ab.astype(BF16), pe8.astype(BF16), w1.astype(BF16), w2x.astype(BF16)


def _importance_matrix(n_cmp, n_cand):
    i = np.arange(n_cmp)[:, None]
    j = np.arange(n_cand)[None, :]
    a = (i >= CMP_RATIO * j - 1) & (i <= CMP_RATIO * j + CMP_RATIO - 1)
    return jnp.asarray(a.astype(np.float32)).astype(BF16)


def _compress_rows(rows, cw):
    w_ab, pe8, w1b, w2x = cw
    nb = rows.shape[0]
    ident = jnp.arange(nb, dtype=jnp.int32).reshape(nb, 1)
    return cmp_fin(cmp_ab(rows, ident, w_ab, pages_per_step=1), None, pe8, w1b, w2x)


def _compress_paged(pool, page_table, tail, cw, *, pages_per_step):
    w_ab, pe8, w1b, w2x = cw
    ab = cmp_ab_paged(pool, page_table, w_ab, pages_per_step=pages_per_step)
    one = jnp.zeros((1, 1), jnp.int32)
    ab_tail = cmp_ab(tail.reshape(1, -1, SUB_W), one, w_ab, pages_per_step=1)
    ab_tail = ab_tail.reshape(tail.shape[0], tail.shape[1], -1)
    return cmp_fin(ab, ab_tail, pe8, w1b, w2x)


def _position_major(a_t):
    n, _, p = a_t.shape
    return jnp.transpose(a_t.reshape(n, N_KV, HEAD_DIM, p), (0, 3, 1, 2))


def kernel(x_prompt, x_sample, cache_cmp_k, cache_cmp_v, cache_sel_k, cache_sel_v, cache_win_k,
           cache_win_v, page_table, ffn1_w_in, ffn1_w_out, ln1_g, ln1_b, w_in, cmp_pe_k, cmp_w1_k,
           cmp_w2_k, cmp_pe_v, cmp_w1_v, cmp_w2_v, gmlp_ln_g, gmlp_ln_b, spatial_w, spatial_b,
           w_branch_a, w_branch_b, w_out, ln2_g, ln2_b, ffn2_w_in, ffn2_w_out, ln3_g, ln3_b):
    nb, t, d = x_prompt.shape
    db, n_new, _ = x_sample.shape
    assert n_new == 1 and t % SEL_TILE == 0
    n_pages = page_table.shape[1]
    past = n_pages * PAGE_SIZE
    n_phys = cache_cmp_k.shape[0]
    m = nb * t

    f1_in, f1_out = ffn1_w_in.astype(BF16), ffn1_w_out.astype(BF16)
    f2_in, f2_out = ffn2_w_in.astype(BF16), ffn2_w_out.astype(BF16)
    proj_w = _split_w_in(w_in)
    wa = _expand_branch_a(w_branch_a)
    wb = w_branch_b.astype(BF16)
    wo = w_out.astype(BF16)
    cw_k = _cmp_weights(cmp_pe_k, cmp_w1_k, cmp_w2_k)
    cw_v = _cmp_weights(cmp_pe_v, cmp_w1_v, cmp_w2_v)

    h = ffn_ln(x_prompt.reshape(m, d), f1_in, f1_out, ln1_g, ln1_b, tm=1024)
    (qt, kct, vct, kst, vst, kwt, vwt, kcb, vcb, ksb, kwb, vstb, vwtb, gtt, uv,
     sgab) = mixer_proj_t(h, *proj_w, nb=nb, tm=512)
    n_cmp = t // CMP_STRIDE
    kcc = _compress_rows(kcb.reshape(nb, n_cmp, SUB_W), cw_k)
    vcc = _compress_rows(vcb.reshape(nb, n_cmp, SUB_W), cw_v)
    imp_t = _importance_matrix(n_cmp, t // SEL_BLOCK).T
    o_a = nsa_prompt(qt, gtt, kcc, jnp.swapaxes(vcc, 1, 2), ksb.reshape(nb, t, SLAB), vstb,
                     kwb.reshape(nb, t, SLAB), vwtb, imp_t)
    sb_tile = jnp.repeat(spatial_b.T, GROUP_CH, axis=1)
    h2 = merge_ln(h, o_a.reshape(m, Q_EXP), uv, sgab, wa, wb, wo, ln2_g, ln2_b, gmlp_ln_g,
                  gmlp_ln_b, spatial_w, sb_tile, tm=256, chunked=True)
    y_prompt = ffn_ln(h2, f2_in, f2_out, ln3_g, ln3_b, tm=1024).reshape(nb, t, d)
    wkeep = min(WINDOW, t)
    p_outs = tuple(_position_major(a) for a in
                   (kct, vct, kst, vst, kwt[:, :, t - wkeep:], vwt[:, :, t - wkeep:]))

    hs = ffn_ln(x_sample.reshape(db, d), f1_in, f1_out, ln1_g, ln1_b, tm=db)
    (qs, kcs, vcs, kss, vss, kws, vws, _, gts, uvs, sgabs) = mixer_proj(hs, *proj_w, tm=db)
    t_pad = -(-(past + n_new) // SEL_BLOCK) * SEL_BLOCK
    n_tail = (t_pad - past) // CMP_STRIDE
    tail_rows = 16

    def tail_of(new):
        flat = jnp.pad(new, ((0, 0), (0, tail_rows * SUB_W - SLAB)))
        return flat.reshape(db, tail_rows, SUB_W)

    assert n_tail <= tail_rows
    pps = min(64, n_pages)
    pool_rows = lambda p: p.reshape(n_phys, PAGE_SIZE, SLAB)
    kcc_s = _compress_paged(pool_rows(cache_cmp_k), page_table, tail_of(kcs), cw_k,
                            pages_per_step=pps)
    vcc_s = _compress_paged(pool_rows(cache_cmp_v), page_table, tail_of(vcs), cw_v,
                            pages_per_step=pps)
    n_sel_s = t_pad // SEL_BLOCK
    n_cand = -(-n_sel_s // 128) * 128
    imp_s = _importance_matrix(kcc_s.shape[1], n_cand)
    q8 = qs.reshape(db, N_HEADS, SLAB)
    o_cmp_s, idx_s = nsa_sample_cmp(q8, kcc_s, vcc_s, imp_s, qpos=past, n_sel=n_sel_s)
    idx_flat = idx_s[:, :N_KV, :SEL_TOP].reshape(db, N_KV * SEL_TOP)
    gt8 = jnp.pad(gts[:, :GATE_W].reshape(db, N_HEADS, 3), ((0, 0), (0, 0), (0, 125)))
    row3 = lambda a: a.reshape(db, 1, SLAB)
    win3 = lambda a: a.reshape(db, -1, SLAB)
    o_as = nsa_sample_sel(page_table, idx_flat, pool_rows(cache_sel_k), pool_rows(cache_sel_v),
                          q8, gt8, o_cmp_s, row3(kss), row3(vss), win3(cache_win_k),
                          win3(cache_win_v), row3(kws), row3(vws), qpos=past)
    sw_row = jnp.repeat(spatial_w[:, 0, 0], GROUP_CH).reshape(1, D_B)
    sb_row = jnp.repeat(spatial_b[:, 0], GROUP_CH).reshape(1, D_B)
    hs2, vn_s = merge_ln(hs, o_as.reshape(db, Q_EXP), uvs, sgabs, wa, wb, wo, ln2_g, ln2_b,
                         gmlp_ln_g, gmlp_ln_b, sw_row, sb_row, tm=db, chunked=False)
    y_sample = ffn_ln(hs2, f2_in, f2_out, ln3_g, ln3_b, tm=db).reshape(db, n_new, d)
    kv4s = lambda a: a.reshape(db, n_new, N_KV, HEAD_DIM)
    wbuf = cache_win_k.shape[1]
    s_win_k = jnp.concatenate([cache_win_k, kv4s(kws)], axis=1)[:, -wbuf:]
    s_win_v = jnp.concatenate([cache_win_v, kv4s(vws)], axis=1)[:, -wbuf:]

    return (y_prompt, y_sample) + p_outs + (kv4s(kcs), kv4s(vcs), kv4s(kss), kv4s(vss),
                                            s_win_k, s_win_v, vn_s.reshape(db, n_new, D_B))
```

```python
import functools

import numpy as np
import jax
import jax.numpy as jnp
from jax import lax
from jax.experimental import pallas as pl
from jax.experimental.pallas import tpu as pltpu

F32 = jnp.float32
BF16 = jnp.bfloat16

D_MODEL = 1024
N_HEADS = 8
HEAD_DIM = 64
N_KV = 2
GROUP = N_HEADS // N_KV
PAGE_SIZE = 128
CMP_LEN = 32
CMP_STRIDE = 16
CMP_HIDDEN = 128
SEL_BLOCK = 64
CMP_RATIO = SEL_BLOCK // CMP_STRIDE
SEL_TOP = 16
WINDOW = 512
Q_BLOCK = 128
FORCE_SCORE = 1e4
D_B = 512
N_GROUPS_B = 4
GROUP_CH = D_B // N_GROUPS_B
CHUNK = 128
D_FF = 2816
DEPTH = 1
ALPHA = (2.0 * DEPTH) ** 0.25
LN_EPS = 1e-5
NEG_INF = -1e30
REMOVED = -3e38
Q_W = N_HEADS * HEAD_DIM
KV_W = N_KV * HEAD_DIM
GATE_W = N_HEADS * 3
SLAB = N_KV * HEAD_DIM
Q_EXP = N_HEADS * SLAB
SUB_W = CMP_STRIDE * SLAB
VMEM_LIMIT = 40 * 1024 * 1024
SEL_TILE = 1024


def _params(sem, vmem=VMEM_LIMIT, flags=None):
    return pltpu.CompilerParams(dimension_semantics=sem, vmem_limit_bytes=vmem, flags=flags)


def _layer_norm(x, g, b):
    mu = jnp.mean(x, axis=-1, keepdims=True)
    xc = x - mu
    var = jnp.mean(xc * xc, axis=-1, keepdims=True)
    return xc * lax.rsqrt(var + LN_EPS) * g + b


def _dot(a, b):
    return jnp.dot(a, b, preferred_element_type=F32)


def _dot_nt(a, b):
    return lax.dot_general(a, b, (((1,), (1,)), ((), ())), preferred_element_type=F32)


def _split3(x):
    hi = x.astype(BF16)
    r1 = x - hi.astype(F32)
    mid = r1.astype(BF16)
    lo = (r1 - mid.astype(F32)).astype(BF16)
    return hi, mid, lo


def _dot_exact_rhs(x, a_bf16):
    hi, mid, lo = _split3(x)
    return _dot(hi, a_bf16) + _dot(mid, a_bf16) + _dot(lo, a_bf16)


def _ffn_ln_kernel(x_ref, wg_ref, wu_ref, wo_ref, g_ref, b_ref, o_ref, xb_ref, acc_ref, *, n_f):
    f = pl.program_id(1)

    @pl.when(f == 0)
    def _():
        xb_ref[...] = x_ref[...].astype(BF16)
        acc_ref[...] = jnp.zeros_like(acc_ref)

    xb = xb_ref[...]
    gate = _dot(xb, wg_ref[...])
    up = _dot(xb, wu_ref[...])
    hid = (gate * jax.nn.sigmoid(gate)) * up
    acc_ref[...] += _dot(hid.astype(BF16), wo_ref[...])

    @pl.when(f == n_f - 1)
    def _():
        y = ALPHA * x_ref[...] + 0.5 * acc_ref[...]
        o_ref[...] = _layer_norm(y, g_ref[...], b_ref[...])


def ffn_ln(x, w_in_b, w_out_b, g, b, *, tm, tf=256):
    m, d = x.shape
    d_ff = w_out_b.shape[0]
    n_f = d_ff // tf
    return pl.pallas_call(
        functools.partial(_ffn_ln_kernel, n_f=n_f),
        grid=(m // tm, n_f),
        in_specs=[
            pl.BlockSpec((tm, d), lambda i, f: (i, 0)),
            pl.BlockSpec((d, tf), lambda i, f: (0, f)),
            pl.BlockSpec((d, tf), lambda i, f: (0, f + n_f)),
            pl.BlockSpec((tf, d), lambda i, f: (f, 0)),
            pl.BlockSpec((1, d), lambda i, f: (0, 0)),
            pl.BlockSpec((1, d), lambda i, f: (0, 0)),
        ],
        out_specs=pl.BlockSpec((tm, d), lambda i, f: (i, 0)),
        out_shape=jax.ShapeDtypeStruct((m, d), F32),
        scratch_shapes=[pltpu.VMEM((tm, d), BF16), pltpu.VMEM((tm, d), F32)],
        compiler_params=_params(("parallel", "arbitrary")),
        name="ffn_ln",
    )(x, w_in_b, w_in_b, w_out_b, g.reshape(1, d), b.reshape(1, d))


def _mixer_proj_kernel(h_ref, wq_ref, wkv_ref, wgt_ref, wuv_ref, wgab_ref,
                       q_ref, kc_ref, vc_ref, ks_ref, vs_ref, kw_ref, vw_ref,
                       kvb_ref, gt_ref, uv_ref, sgab_ref):
    hb = h_ref[...].astype(BF16)
    q_ref[...] = (_dot(hb, wq_ref[...]) * (HEAD_DIM ** -0.5)).astype(BF16)
    kv = _dot(hb, wkv_ref[...])
    for j, ref in enumerate((kc_ref, vc_ref, ks_ref, vs_ref, kw_ref, vw_ref)):
        ref[...] = kv[:, j * SLAB:(j + 1) * SLAB]
    kvb_ref[...] = kv.astype(BF16)
    gt_ref[...] = jax.nn.sigmoid(_dot(hb, wgt_ref[...]))
    uv_ref[...] = jax.nn.gelu(_dot(hb, wuv_ref[...]))
    sgab_ref[...] = jax.nn.sigmoid(_dot(hb, wgab_ref[...])).astype(BF16)


def mixer_proj(h, wq, wkv, wgt, wuv, wgab, *, tm):
    m, d = h.shape
    row = lambda n: pl.BlockSpec((tm, n), lambda i: (i, 0))
    full = lambda w: pl.BlockSpec(w.shape, lambda i: (0, 0))
    out_shape = (
        [jax.ShapeDtypeStruct((m, Q_EXP), BF16)]
        + [jax.ShapeDtypeStruct((m, SLAB), F32)] * 6
        + [jax.ShapeDtypeStruct((m, 6 * SLAB), BF16),
           jax.ShapeDtypeStruct((m, 128), F32),
           jax.ShapeDtypeStruct((m, 2 * D_B), F32),
           jax.ShapeDtypeStruct((m, 2 * D_MODEL), BF16)])
    out_specs = ([row(Q_EXP)] + [row(SLAB)] * 6
                 + [row(6 * SLAB), row(128), row(2 * D_B), row(2 * D_MODEL)])
    return pl.pallas_call(
        _mixer_proj_kernel,
        grid=(m // tm,),
        in_specs=[row(d), full(wq), full(wkv), full(wgt), full(wuv), full(wgab)],
        out_specs=out_specs,
        out_shape=out_shape,
        compiler_params=_params(("parallel",)),
        name="mixer_proj",
    )(h, wq, wkv, wgt, wuv, wgab)


def _mixer_proj_t_kernel(h_ref, wq_ref, wkv_ref, wgt_ref, wuv_ref, wgab_ref,
                         qt_ref, kct_ref, vct_ref, kst_ref, vst_ref, kwt_ref, vwt_ref,
                         kcb_ref, vcb_ref, ksb_ref, kwb_ref, vstb_ref, vwtb_ref,
                         gtt_ref, uv_ref, sgab_ref):
    hb = h_ref[...].astype(BF16)
    q = _dot(hb, wq_ref[...]) * (HEAD_DIM ** -0.5)
    for h in range(N_HEADS):
        qt_ref[h * SLAB:(h + 1) * SLAB, :] = q[:, h * SLAB:(h + 1) * SLAB].T.astype(BF16)
    kv = _dot(hb, wkv_ref[...])
    slabs = [kv[:, j * SLAB:(j + 1) * SLAB] for j in range(6)]
    for slab, ref in zip(slabs, (kct_ref, vct_ref, kst_ref, vst_ref, kwt_ref, vwt_ref)):
        ref[...] = slab.T
    kcb_ref[...] = slabs[0].astype(BF16)
    vcb_ref[...] = slabs[1].astype(BF16)
    ksb_ref[...] = slabs[2].astype(BF16)
    kwb_ref[...] = slabs[4].astype(BF16)
    vstb_ref[...] = slabs[3].T.astype(BF16)
    vwtb_ref[...] = slabs[5].T.astype(BF16)
    gtt_ref[...] = jax.nn.sigmoid(_dot(hb, wgt_ref[...])).T
    uv_ref[...] = jax.nn.gelu(_dot(hb, wuv_ref[...]))
    sgab_ref[...] = jax.nn.sigmoid(_dot(hb, wgab_ref[...])).astype(BF16)


def mixer_proj_t(h, wq, wkv, wgt, wuv, wgab, *, nb, tm):
    m, d = h.shape
    t = m // nb
    per_b = t // tm
    row = lambda n: pl.BlockSpec((tm, n), lambda i: (i, 0))
    col = lambda n: pl.BlockSpec((None, n, tm), lambda i: (i // per_b, 0, i % per_b))
    full = lambda w: pl.BlockSpec(w.shape, lambda i: (0, 0))
    sds = jax.ShapeDtypeStruct
    out_shape = ([sds((nb, Q_EXP, t), BF16)] + [sds((nb, SLAB, t), F32)] * 6
                 + [sds((m, SLAB), BF16)] * 4 + [sds((nb, SLAB, t), BF16)] * 2
                 + [sds((nb, 128, t), F32), sds((m, 2 * D_B), F32), sds((m, 2 * D_MODEL), BF16)])
    out_specs = ([col(Q_EXP)] + [col(SLAB)] * 6 + [row(SLAB)] * 4 + [col(SLAB)] * 2
                 + [col(128), row(2 * D_B), row(2 * D_MODEL)])
    return pl.pallas_call(
        _mixer_proj_t_kernel,
        grid=(m // tm,),
        in_specs=[row(d), full(wq), full(wkv), full(wgt), full(wuv), full(wgab)],
        out_specs=out_specs,
        out_shape=out_shape,
        compiler_params=_params(("parallel",)),
        name="mixer_proj_t",
    )(h, wq, wkv, wgt, wuv, wgab)


def _cmp_ab_kernel(pt_ref, *refs, n_pages):
    del pt_ref
    page_refs = refs[:n_pages]
    w_ref = refs[n_pages]
    o_ref = refs[n_pages + 1]
    if n_pages == 1:
        x = page_refs[0][...]
    else:
        x = jnp.concatenate([r[...] for r in page_refs], axis=0)
    o_ref[...] = _dot(x.astype(BF16), w_ref[...])


def cmp_ab(pool, page_table, w_ab, *, pages_per_step):
    nb, n_pages = page_table.shape
    rows = pool.shape[1]
    pp = pages_per_step
    n_steps = n_pages // pp

    def page_spec(k):
        return pl.BlockSpec((None, rows, SUB_W), lambda b, s, pt: (pt[b, s * pp + k], 0, 0))

    grid_spec = pltpu.PrefetchScalarGridSpec(
        num_scalar_prefetch=1,
        grid=(nb, n_steps),
        in_specs=[page_spec(k) for k in range(pp)]
        + [pl.BlockSpec(w_ab.shape, lambda b, s, pt: (0, 0))],
        out_specs=pl.BlockSpec((None, pp * rows, w_ab.shape[1]), lambda b, s, pt: (b, s, 0)),
    )
    return pl.pallas_call(
        functools.partial(_cmp_ab_kernel, n_pages=pp),
        grid_spec=grid_spec,
        out_shape=jax.ShapeDtypeStruct((nb, n_pages * rows, w_ab.shape[1]), F32),
        compiler_params=_params(("parallel", "arbitrary")),
        name="cmp_ab",
    )(page_table, *([pool] * pp), w_ab)


def _cmp_ab_paged_kernel(pt_ref, *refs, n_pages):
    del pt_ref
    page_refs = refs[:n_pages]
    w_ref, o_ref, rows_ref = refs[n_pages:]
    for k, r in enumerate(page_refs):
        rows_ref[k * PAGE_SIZE:(k + 1) * PAGE_SIZE, :] = r[...].T
    n_sub = n_pages * PAGE_SIZE // CMP_STRIDE
    x = jnp.concatenate([rows_ref[pl.ds(p, n_sub, stride=CMP_STRIDE), :].astype(BF16)
                         for p in range(CMP_STRIDE)], axis=1)
    o_ref[...] = _dot(x, w_ref[...])


def cmp_ab_paged(pool, page_table, w_ab, *, pages_per_step):
    nb, n_pages = page_table.shape
    pp = pages_per_step
    sub_per_page = PAGE_SIZE // CMP_STRIDE

    n_phys = pool.shape[0]

    def page_spec(k):
        return pl.BlockSpec((None, SLAB, PAGE_SIZE),
                            lambda b, s, pt: (jnp.clip(pt[b, s * pp + k], 0, n_phys - 1), 0, 0))

    grid_spec = pltpu.PrefetchScalarGridSpec(
        num_scalar_prefetch=1,
        grid=(nb, n_pages // pp),
        in_specs=[page_spec(k) for k in range(pp)]
        + [pl.BlockSpec(w_ab.shape, lambda b, s, pt: (0, 0))],
        out_specs=pl.BlockSpec((None, pp * sub_per_page, w_ab.shape[1]), lambda b, s, pt: (b, s, 0)),
        scratch_shapes=[pltpu.VMEM((pp * PAGE_SIZE, SLAB), F32)],
    )
    return pl.pallas_call(
        functools.partial(_cmp_ab_paged_kernel, n_pages=pp),
        grid_spec=grid_spec,
        out_shape=jax.ShapeDtypeStruct((nb, n_pages * sub_per_page, w_ab.shape[1]), F32),
        compiler_params=_params(("parallel", "arbitrary")),
        name="cmp_ab_paged",
    )(page_table, *([pool] * pp), w_ab)


def _cmp_fin_kernel(*refs, has_tail):
    if has_tail:
        ab_ref, tail_ref, pe_ref, w1_ref, w2_ref, o_ref = refs
    else:
        ab_ref, pe_ref, w1_ref, w2_ref, o_ref = refs
    hw = 2 * CMP_HIDDEN
    n = ab_ref.shape[0]
    pe_h = _dot(pe_ref[...], w1_ref[...])[0:1, :]
    bias = jnp.concatenate([pe_h, pe_h], axis=1)
    w2 = w2_ref[...]
    first = ab_ref[:, 0:hw]
    second = pltpu.roll(ab_ref[:, hw:2 * hw], n - 1, 0)
    if has_tail:
        t_first = tail_ref[:, 0:hw]
        t_second = tail_ref[:, hw:2 * hw]
        nt = tail_ref.shape[0]
        is_last = lax.broadcasted_iota(jnp.int32, (n, 1), 0) == n - 1
        second = jnp.where(is_last, t_second[0:1, :], second)
        t_hid = t_first + pltpu.roll(t_second, nt - 1, 0) + bias
        o_ref[n:n + nt, :] = _dot(jax.nn.gelu(t_hid).astype(BF16), w2).astype(BF16)
        n_out = o_ref.shape[0]
        if n_out > n + nt:
            o_ref[n + nt:n_out, :] = jnp.zeros((n_out - n - nt, SLAB), BF16)
    hid = first + second + bias
    o_ref[0:n, :] = _dot(jax.nn.gelu(hid).astype(BF16), w2).astype(BF16)


def cmp_fin(ab, tail, pe8, w1b, w2x):
    nb, n, w = ab.shape
    has_tail = tail is not None
    nt = tail.shape[1] if has_tail else 0
    full = lambda a: pl.BlockSpec(a.shape, lambda b: (0, 0))
    in_specs = [pl.BlockSpec((None, n, w), lambda b: (b, 0, 0))]
    args = [ab]
    if has_tail:
        in_specs.append(pl.BlockSpec((None, nt, w), lambda b: (b, 0, 0)))
        args.append(tail)
    in_specs += [full(pe8), full(w1b), full(w2x)]
    args += [pe8, w1b, w2x]
    n_out = -(-(n + nt) // 128) * 128 if has_tail else n
    return pl.pallas_call(
        functools.partial(_cmp_fin_kernel, has_tail=has_tail),
        grid=(nb,),
        in_specs=in_specs,
        out_specs=pl.BlockSpec((None, n_out, SLAB), lambda b: (b, 0, 0)),
        out_shape=jax.ShapeDtypeStruct((nb, n_out, SLAB), BF16),
        compiler_params=_params(("parallel",)),
        name="cmp_fin",
    )(*args)


def _slope_of_head(h):
    return 2.0 ** (-(h + 1))


def _slope_column():
    head = lax.broadcasted_iota(jnp.int32, (N_HEADS, 1), 0)
    slope = jnp.zeros((N_HEADS, 1), F32)
    for h in range(N_HEADS):
        slope = jnp.where(head == h, _slope_of_head(h), slope)
    return slope


def _top_k_mask_t(score_t, k):
    n = score_t.shape[0]
    cand_idx = lax.broadcasted_iota(jnp.int32, score_t.shape, 0).astype(F32)

    def body(_, carry):
        s, sel = carry
        m = jnp.max(s, axis=0, keepdims=True)
        first = jnp.min(jnp.where(s == m, cand_idx, float(n)), axis=0, keepdims=True)
        pick = cand_idx == first
        return jnp.where(pick, REMOVED, s), jnp.where(pick, 1.0, sel)

    _, sel = lax.fori_loop(0, k, body, (score_t, jnp.zeros_like(score_t)), unroll=True)
    return sel


def _softmax_masked(s, mask):
    s = jnp.where(mask, s, NEG_INF)
    m = jnp.max(s, axis=-1, keepdims=True)
    e = jnp.exp(s - m)
    return jnp.where(mask, e / jnp.sum(e, axis=-1, keepdims=True), 0.0)


def _softmax_masked_t(s, mask):
    s = jnp.where(mask, s, NEG_INF)
    m = jnp.max(s, axis=0, keepdims=True)
    e = jnp.exp(s - m)
    return jnp.where(mask, e * (1.0 / jnp.sum(e, axis=0, keepdims=True)), 0.0)


def _nsa_prompt_kernel(qt_ref, gtt_ref, kc_ref, vct_ref, ks_ref, vst_ref, kw_ref, vwt_ref, impt_ref,
                       o_ref, selt_ref, bias_c_ref, bias_w_ref, bias_s_ref):
    qb = pl.program_id(1)
    s0 = qb * Q_BLOCK
    nq = Q_BLOCK
    hq = N_HEADS * nq
    nc = kc_ref.shape[0]
    n_sel = ks_ref.shape[0] // SEL_BLOCK
    tk = SEL_TILE
    assert HEAD_DIM == 64 and SEL_BLOCK == 64 and Q_BLOCK == 128

    lane = lax.broadcasted_iota(jnp.int32, (1, hq), 1)
    q_loc = lane & (nq - 1)
    qpos_f = (s0 + q_loc).astype(F32)
    qpos_q = s0 + lax.broadcasted_iota(jnp.int32, (1, nq), 1)
    row_half = lax.broadcasted_iota(jnp.int32, (SLAB, 1), 0) >> 6
    slope = jnp.zeros((1, hq), F32)
    for h in range(N_HEADS):
        slope = jnp.where((lane >> 7) == h, _slope_of_head(h), slope)
    n_win = WINDOW + nq
    k_loc = lax.broadcasted_iota(jnp.int32, (tk, 1), 0)

    @pl.when(qb == 0)
    def _():
        q_loc_f = q_loc.astype(F32)
        c_end = lax.broadcasted_iota(jnp.int32, (nc, 1), 0) * CMP_STRIDE + (CMP_LEN - 1)
        bias_c_ref[...] = slope * (q_loc_f - c_end.astype(F32))
        w_loc = lax.broadcasted_iota(jnp.int32, (n_win, 1), 0)
        bias_w_ref[...] = slope * (q_loc_f - w_loc.astype(F32))
        bias_s_ref[...] = slope * k_loc.astype(F32)

    j_idx = lax.broadcasted_iota(jnp.int32, (n_sel, 1), 0)
    cur = qpos_q >> 6
    forced = (j_idx == 0) | (j_idx == cur) | (j_idx == cur - 1)
    valid = (j_idx * SEL_BLOCK) <= qpos_q

    w0 = pl.multiple_of(jnp.maximum(s0 - WINDOW, 0), Q_BLOCK)
    kw = kw_ref[pl.ds(w0, n_win), :]
    vwt = vwt_ref[:, pl.ds(w0, n_win)]

    assert tk % nq == 0
    n_full = s0 // tk
    blocks_per_tile = tk // SEL_BLOCK

    qt = jnp.concatenate([qt_ref[h * SLAB:(h + 1) * SLAB, :] for h in range(N_HEADS)],
                         axis=1)

    bias_c = bias_c_ref[...]
    mask_c = bias_c >= slope * (-s0).astype(F32)
    p = _softmax_masked_t(_dot(kc_ref[...], qt) - bias_c, mask_c)
    o_cmp = _dot(vct_ref[...], p.astype(BF16))

    imp_t = impt_ref[...]
    scores = []
    for g in range(N_KV):
        p_sum = p[:, g * GROUP * nq:(g * GROUP + 1) * nq]
        for r in range(1, GROUP):
            p_sum = p_sum + p[:, (g * GROUP + r) * nq:(g * GROUP + r + 1) * nq]
        hi, mid, lo = _split3(p_sum)
        p_slc = _dot(imp_t, hi) + _dot(imp_t, mid) + _dot(imp_t, lo)
        scores.append(jnp.where(valid, jnp.where(forced, FORCE_SCORE, p_slc), NEG_INF))
    sel = _top_k_mask_t(jnp.concatenate(scores, axis=1), min(SEL_TOP, n_sel))
    for g in range(N_KV):
        selt_ref[g] = sel[:, g * nq:(g + 1) * nq]


    def sel_tile(t, carry, diagonal):
        m_old, l_old, acc_old = carry
        kt = pl.multiple_of(t * tk, tk)
        k = ks_ref[pl.ds(kt, tk), :]
        vt = vst_ref[:, pl.ds(kt, tk)]
        s_rel = _dot(k, qt) + bias_s_ref[...]
        shift = slope * (kt.astype(F32) - qpos_f)
        cols = []
        for g in range(N_KV):
            picked = jnp.concatenate(
                [jnp.broadcast_to(selt_ref[g, pl.ds(t * blocks_per_tile + jj, 1), :],
                                  (SEL_BLOCK, nq))
                 for jj in range(blocks_per_tile)], axis=0) > 0.5
            if diagonal:
                picked = picked & ((kt + k_loc) <= qpos_q)
            for r in range(GROUP):
                h = g * GROUP + r
                cols.append(jnp.where(picked, s_rel[:, h * nq:(h + 1) * nq], NEG_INF))
        s_rel = jnp.concatenate(cols, axis=1)
        m_new = jnp.maximum(m_old, jnp.max(s_rel, axis=0, keepdims=True) + shift)
        alpha = jnp.exp(m_old - m_new)
        p_t = jnp.exp(s_rel - (m_new - shift))
        l_new = alpha * l_old + jnp.sum(p_t, axis=0, keepdims=True)
        acc_new = alpha * acc_old + _dot(vt, p_t.astype(BF16))
        return m_new, l_new, acc_new

    init = (jnp.full((1, hq), NEG_INF, F32), jnp.zeros((1, hq), F32), jnp.zeros((SLAB, hq), F32))
    state = lax.fori_loop(0, n_full, functools.partial(sel_tile, diagonal=False), init)
    _, l_sel, acc_sel = sel_tile(n_full, state, diagonal=True)
    o_sel = acc_sel * (1.0 / l_sel)

    bias_w = bias_w_ref[...]
    lead = (s0 - w0).astype(F32)
    mask_w = (bias_w >= slope * (-lead)) & (bias_w < slope * (WINDOW - lead))
    o_win = _dot(vwt, _softmax_masked_t(_dot(kw, qt) - bias_w, mask_w).astype(BF16))

    for h in range(N_HEADS):
        cols = slice(h * nq, (h + 1) * nq)
        o = (gtt_ref[3 * h:3 * h + 1, :] * o_cmp[:, cols]
             + gtt_ref[3 * h + 1:3 * h + 2, :] * o_sel[:, cols]
             + gtt_ref[3 * h + 2:3 * h + 3, :] * o_win[:, cols])
        o_ref[:, h * SLAB:(h + 1) * SLAB] = jnp.where(row_half == h // GROUP, o, 0.0).T.astype(BF16)


def nsa_prompt(qt, gtt, kc, vct, ks, vst, kw, vwt, imp_t):
    nb, _, t = qt.shape
    nc = kc.shape[1]
    rows = pl.BlockSpec((None, t, SLAB), lambda b, i: (b, 0, 0))
    cols = pl.BlockSpec((None, SLAB, t), lambda b, i: (b, 0, 0))
    return pl.pallas_call(
        _nsa_prompt_kernel,
        grid=(nb, t // Q_BLOCK),
        in_specs=[
            pl.BlockSpec((None, Q_EXP, Q_BLOCK), lambda b, i: (b, 0, i)),
            pl.BlockSpec((None, 128, Q_BLOCK), lambda b, i: (b, 0, i)),
            pl.BlockSpec((None, nc, SLAB), lambda b, i: (b, 0, 0)),
            pl.BlockSpec((None, SLAB, nc), lambda b, i: (b, 0, 0)),
            rows, cols, rows, cols,
            pl.BlockSpec(imp_t.shape, lambda b, i: (0, 0)),
        ],
        out_specs=pl.BlockSpec((None, Q_BLOCK, Q_EXP), lambda b, i: (b, i, 0)),
        out_shape=jax.ShapeDtypeStruct((nb, t, Q_EXP), BF16),
        scratch_shapes=[pltpu.VMEM((N_KV, t // SEL_BLOCK, Q_BLOCK), F32),
                        pltpu.VMEM((nc, N_HEADS * Q_BLOCK), F32),
                        pltpu.VMEM((WINDOW + Q_BLOCK, N_HEADS * Q_BLOCK), F32),
                        pltpu.VMEM((SEL_TILE, N_HEADS * Q_BLOCK), F32)],
        compiler_params=_params(("parallel", "arbitrary")),
        name="nsa_prompt",
    )(qt, gtt, kc, vct, ks, vst, kw, vwt, imp_t)


def _nsa_sample_cmp_kernel(q_ref, kc_ref, vc_ref, imp_ref, oc_ref, idx_ref, *, qpos, n_sel):
    nc = kc_ref.shape[0]
    n_cand = imp_ref.shape[1]
    q = q_ref[...]
    head = lax.broadcasted_iota(jnp.int32, (N_HEADS, 1), 0)
    slope = _slope_column()
    c_end = lax.broadcasted_iota(jnp.int32, (1, nc), 1) * CMP_STRIDE + (CMP_LEN - 1)
    dist = qpos - c_end
    mask = dist >= 0
    s = _dot_nt(q, kc_ref[...]) - slope * dist.astype(F32)
    p = _softmax_masked(s, mask)
    oc_ref[...] = _dot(p.astype(BF16), vc_ref[...])

    p_sum = jnp.zeros_like(p)
    for g in range(N_KV):
        grp = jnp.sum(p[g * GROUP:(g + 1) * GROUP], axis=0, keepdims=True)
        p_sum = jnp.where(head == g, grp, p_sum)
    p_slc = _dot_exact_rhs(p_sum, imp_ref[...])

    j_idx = lax.broadcasted_iota(jnp.int32, (1, n_cand), 1)
    cur = qpos // SEL_BLOCK
    forced = (j_idx == 0) | (j_idx == cur) | (j_idx == cur - 1)
    valid = (j_idx * SEL_BLOCK) <= qpos
    score = jnp.where(valid, jnp.where(forced, FORCE_SCORE, p_slc), NEG_INF)
    score = jnp.where(j_idx < n_sel, score, REMOVED)

    cand = lax.broadcasted_iota(jnp.int32, score.shape, 1).astype(F32)
    out_lane = lax.broadcasted_iota(jnp.int32, (N_HEADS, 128), 1)

    def body(k, carry):
        sc, out = carry
        m = jnp.max(sc, axis=1, keepdims=True)
        first = jnp.min(jnp.where(sc == m, cand, float(n_cand)), axis=1, keepdims=True)
        return (jnp.where(cand == first, REMOVED, sc),
                jnp.where(out_lane == k, first.astype(jnp.int32), out))

    _, out = lax.fori_loop(0, SEL_TOP, body, (score, jnp.zeros((N_HEADS, 128), jnp.int32)),
                           unroll=True)
    idx_ref[...] = out


def nsa_sample_cmp(q8, kc, vc, imp, *, qpos, n_sel):
    nb = q8.shape[0]
    nc = kc.shape[1]
    return pl.pallas_call(
        functools.partial(_nsa_sample_cmp_kernel, qpos=qpos, n_sel=n_sel),
        grid=(nb,),
        in_specs=[
            pl.BlockSpec((None, N_HEADS, SLAB), lambda b: (b, 0, 0)),
            pl.BlockSpec((None, nc, SLAB), lambda b: (b, 0, 0)),
            pl.BlockSpec((None, nc, SLAB), lambda b: (b, 0, 0)),
            pl.BlockSpec(imp.shape, lambda b: (0, 0)),
        ],
        out_specs=[pl.BlockSpec((None, N_HEADS, SLAB), lambda b: (b, 0, 0)),
                   pl.BlockSpec((None, N_HEADS, 128), lambda b: (b, 0, 0))],
        out_shape=[jax.ShapeDtypeStruct((nb, N_HEADS, SLAB), F32),
                   jax.ShapeDtypeStruct((nb, N_HEADS, 128), jnp.int32)],
        compiler_params=_params(("parallel",)),
        name="nsa_sample_cmp",
    )(q8, kc, vc, imp)


def _nsa_sample_sel_kernel(pt_ref, idx_ref, *refs, qpos, n_past_blocks):
    del pt_ref
    n_pg = N_KV * SEL_TOP
    k_pages = refs[:n_pg]
    v_pages = refs[n_pg:2 * n_pg]
    (q_ref, gt_ref, oc_ref, kn_ref, vn_ref, wk_ref, wv_ref, wkn_ref, wvn_ref,
     o_ref, kr_ref, vr_ref) = refs[2 * n_pg:]
    b = pl.program_id(0)
    q = q_ref[...]
    head = lax.broadcasted_iota(jnp.int32, (N_HEADS, 1), 0)
    slope = _slope_column()
    grp_of_row = head >> 2
    assert GROUP == 4 and HEAD_DIM == 64 and SEL_BLOCK == 64
    lane_half = lax.broadcasted_iota(jnp.int32, (1, SLAB), 1) >> 6
    blocks_per_page = PAGE_SIZE // SEL_BLOCK
    n_keys = SEL_TOP * SEL_BLOCK
    key_slot = lax.broadcasted_iota(jnp.int32, (1, n_keys), 1) >> 6
    key_off = lax.broadcasted_iota(jnp.int32, (1, n_keys), 1) & (SEL_BLOCK - 1)
    qf = q.astype(F32)

    for i in range(n_pg):
        kr_ref[i * PAGE_SIZE:(i + 1) * PAGE_SIZE, :] = k_pages[i][...].T
        vr_ref[i * PAGE_SIZE:(i + 1) * PAGE_SIZE, :] = v_pages[i][...].T

    s_new = jnp.sum(qf * kn_ref[...], axis=1, keepdims=True)

    o_sel = jnp.zeros((N_HEADS, SLAB), F32)
    for g in range(N_KV):
        ks, vs = [], []
        start = jnp.zeros((1, n_keys), jnp.int32)
        for k in range(SEL_TOP):
            i = g * SEL_TOP + k
            j = idx_ref[b, i]
            off = pl.multiple_of(i * PAGE_SIZE + (j % blocks_per_page) * SEL_BLOCK, SEL_BLOCK)
            ks.append(kr_ref[pl.ds(off, SEL_BLOCK), :])
            vs.append(vr_ref[pl.ds(off, SEL_BLOCK), :])
            start = jnp.where(key_slot == k, j * SEL_BLOCK, start)
        k_all = jnp.concatenate(ks, axis=0).astype(BF16)
        v_all = jnp.concatenate(vs, axis=0).astype(BF16)
        spos = start + key_off
        dist = qpos - spos
        mask = (dist >= 0) & (start < n_past_blocks * SEL_BLOCK)
        s = _dot_nt(q, k_all) - slope * dist.astype(F32)
        s = jnp.where(mask, s, NEG_INF)
        m = jnp.maximum(jnp.max(s, axis=-1, keepdims=True), s_new)
        e = jnp.exp(s - m)
        e_new = jnp.exp(s_new - m)
        l = jnp.sum(e, axis=-1, keepdims=True) + e_new
        o_g = (_dot(e.astype(BF16), v_all) + e_new * vn_ref[...]) / l
        o_sel = jnp.where(grp_of_row == g, o_g, o_sel)

    n_win = wk_ref.shape[0]
    dist_w = n_win - lax.broadcasted_iota(jnp.int32, (1, n_win), 1)
    mask_w = (dist_w >= 0) & (dist_w < WINDOW)
    s_w = _dot_nt(q, wk_ref[...].astype(BF16)) - slope * dist_w.astype(F32)
    s_w = jnp.where(mask_w, s_w, NEG_INF)
    s_wn = jnp.sum(qf * wkn_ref[...], axis=1, keepdims=True)
    m_w = jnp.maximum(jnp.max(s_w, axis=-1, keepdims=True), s_wn)
    e_w = jnp.exp(s_w - m_w)
    e_wn = jnp.exp(s_wn - m_w)
    l_w = jnp.sum(e_w, axis=-1, keepdims=True) + e_wn
    o_win = (_dot(e_w.astype(BF16), wv_ref[...].astype(BF16)) + e_wn * wvn_ref[...]) / l_w

    gt = gt_ref[...]
    o = gt[:, 0:1] * oc_ref[...] + gt[:, 1:2] * o_sel + gt[:, 2:3] * o_win
    o_ref[...] = jnp.where(lane_half == grp_of_row, o, 0.0).astype(BF16)


def nsa_sample_sel(page_table, idx, pool_k, pool_v, q8, gt8, o_cmp, k_new, v_new,
                   win_k, win_v, wk_new, wv_new, *, qpos):
    nb, n_pages = page_table.shape
    n_win = win_k.shape[1]
    n_phys = pool_k.shape[0]
    blocks_per_page = PAGE_SIZE // SEL_BLOCK

    def page_spec(i):
        def index_map(b, pt, ix):
            page = jnp.clip(ix[b, i] // blocks_per_page, 0, n_pages - 1)
            return (jnp.clip(pt[b, page], 0, n_phys - 1), 0, 0)
        return pl.BlockSpec((None, SLAB, PAGE_SIZE), index_map)

    per_b = lambda r: pl.BlockSpec((None, r, SLAB), lambda b, pt, ix: (b, 0, 0))
    n_pg = N_KV * SEL_TOP
    grid_spec = pltpu.PrefetchScalarGridSpec(
        num_scalar_prefetch=2,
        grid=(nb,),
        in_specs=[page_spec(i) for i in range(n_pg)] * 2
        + [per_b(N_HEADS), per_b(N_HEADS), per_b(N_HEADS), per_b(1), per_b(1),
           per_b(n_win), per_b(n_win), per_b(1), per_b(1)],
        out_specs=per_b(N_HEADS),
        scratch_shapes=[pltpu.VMEM((n_pg * PAGE_SIZE, SLAB), F32)] * 2,
    )
    return pl.pallas_call(
        functools.partial(_nsa_sample_sel_kernel, qpos=qpos, n_past_blocks=n_pages * blocks_per_page),
        grid_spec=grid_spec,
        out_shape=jax.ShapeDtypeStruct((nb, N_HEADS, SLAB), BF16),
        compiler_params=_params(("arbitrary",)),
        name="nsa_sample_sel",
    )(page_table, idx, *([pool_k] * n_pg), *([pool_v] * n_pg), q8, gt8, o_cmp, k_new, v_new,
      win_k, win_v, wk_new, wv_new)


def _merge_ln_kernel(*refs, chunked):
    if chunked:
        (h_ref, oa_ref, uv_ref, sg_ref, wa_ref, wb_ref, wo_ref, g2_ref, b2_ref, lg_ref, lb_ref,
         sw_ref, sb_ref, o_ref) = refs
    else:
        (h_ref, oa_ref, uv_ref, sg_ref, wa_ref, wb_ref, wo_ref, g2_ref, b2_ref, lg_ref, lb_ref,
         sw_ref, sb_ref, o_ref, vn_ref) = refs
    tm = h_ref.shape[0]
    u = uv_ref[:, 0:D_B]
    vn = _layer_norm(uv_ref[:, D_B:2 * D_B], lg_ref[...], lb_ref[...])
    if chunked:
        row = lax.broadcasted_iota(jnp.int32, (CHUNK, CHUNK), 0)
        col = lax.broadcasted_iota(jnp.int32, (CHUNK, CHUNK), 1)
        vnb = vn.astype(BF16)
        chunks = []
        for c in range(tm // CHUNK):
            parts = []
            for hg in range(N_GROUPS_B):
                w = jnp.where(col <= row, sw_ref[hg], 0.0).astype(BF16)
                parts.append(_dot(w, vnb[c * CHUNK:(c + 1) * CHUNK, hg * GROUP_CH:(hg + 1) * GROUP_CH]))
            chunks.append(jnp.concatenate(parts, axis=1) + sb_ref[...])
        s = jnp.concatenate(chunks, axis=0)
    else:
        s = vn * sw_ref[...] + sb_ref[...]
        vn_ref[...] = vn
    z = u * s
    branch_a = _dot(oa_ref[...], wa_ref[...])
    branch_b = _dot(z.astype(BF16), wb_ref[...])
    merged = (sg_ref[:, 0:D_MODEL].astype(F32) * branch_a
              + sg_ref[:, D_MODEL:2 * D_MODEL].astype(F32) * branch_b)
    y = ALPHA * h_ref[...] + _dot(merged.astype(BF16), wo_ref[...])
    o_ref[...] = _layer_norm(y, g2_ref[...], b2_ref[...])


def merge_ln(h, oa, uv, sgab, wa, wb, wo, g2, b2, lg, lb, sw, sb, *, tm, chunked):
    m, d = h.shape
    row = lambda n: pl.BlockSpec((tm, n), lambda i: (i, 0))
    full = lambda a: pl.BlockSpec(a.shape, lambda i: (0,) * a.ndim)
    vec = lambda a: a.reshape(1, -1)
    g2, b2, lg, lb = vec(g2), vec(b2), vec(lg), vec(lb)
    out_shape = [jax.ShapeDtypeStruct((m, d), F32)]
    out_specs = [row(d)]
    if not chunked:
        out_shape.append(jax.ShapeDtypeStruct((m, D_B), F32))
        out_specs.append(row(D_B))
    res = pl.pallas_call(
        functools.partial(_merge_ln_kernel, chunked=chunked),
        grid=(m // tm,),
        in_specs=[row(d), row(Q_EXP), row(2 * D_B), row(2 * D_MODEL), full(wa), full(wb), full(wo),
                  full(g2), full(b2), full(lg), full(lb), full(sw), full(sb)],
        out_specs=out_specs,
        out_shape=out_shape,
        compiler_params=_params(("parallel",)),
        name="merge_ln",
    )(h, oa, uv, sgab, wa, wb, wo, g2, b2, lg, lb, sw, sb)
    return res if not chunked else res[0]


def _group_mask():
    return (np.arange(N_HEADS)[:, None] // GROUP == np.arange(N_KV)[None, :]).astype(np.float32)


def _split_w_in(w_in):
    sizes = (Q_W,) + (KV_W,) * 6 + (GATE_W, D_B, D_B, D_MODEL, D_MODEL)
    cuts = np.cumsum(sizes)[:-1].tolist()
    q, kc, vc, ks, vs, kw, vw, g, u, v, ga, gb = jnp.split(w_in, cuts, axis=1)
    d = w_in.shape[0]
    gm = jnp.asarray(_group_mask())
    wq = (q.reshape(d, N_HEADS, 1, HEAD_DIM) * gm[None, :, :, None]).reshape(d, Q_EXP)
    wkv = jnp.concatenate([kc, vc, ks, vs, kw, vw], axis=1)
    wgt = jnp.pad(g, ((0, 0), (0, 128 - GATE_W)))
    wuv = jnp.concatenate([u, v], axis=1)
    wgab = jnp.concatenate([ga, gb], axis=1)
    return tuple(w.astype(BF16) for w in (wq, wkv, wgt, wuv, wgab))


def _expand_branch_a(w_a):
    gm = jnp.asarray(_group_mask())
    w = w_a.reshape(N_HEADS, 1, HEAD_DIM, w_a.shape[1]) * gm[:, :, None, None]
    return w.reshape(Q_EXP, w_a.shape[1]).astype(BF16)


def _cmp_weights(pe, w1, w2):
    n_sub = CMP_LEN // CMP_STRIDE
    w1r = w1.reshape(n_sub, CMP_STRIDE, HEAD_DIM, CMP_HIDDEN)
    eye = jnp.eye(N_KV, dtype=w1.dtype)
    w_ab = jnp.einsum('spdc,gh->pgdshc', w1r, eye).reshape(SUB_W, n_sub * N_KV * CMP_HIDDEN)
    w2x = jnp.einsum('cd,gh->gchd', w2, eye).reshape(N_KV * CMP_HIDDEN, SLAB)
    pe8 = jnp.pad(pe.reshape(1, -1), ((0, 7), (0, 0)))
    return w_ab.astype(BF16), pe8.astype(BF16), w1.astype(BF16), w2x.astype(BF16)


def _importance_matrix(n_cmp, n_cand):
    i = np.arange(n_cmp)[:, None]
    j = np.arange(n_cand)[None, :]
    a = (i >= CMP_RATIO * j - 1) & (i <= CMP_RATIO * j + CMP_RATIO - 1)
    return jnp.asarray(a.astype(np.float32)).astype(BF16)


def _compress_rows(rows, cw):
    w_ab, pe8, w1b, w2x = cw
    nb = rows.shape[0]
    ident = jnp.arange(nb, dtype=jnp.int32).reshape(nb, 1)
    return cmp_fin(cmp_ab(rows, ident, w_ab, pages_per_step=1), None, pe8, w1b, w2x)


def _compress_paged(pool, page_table, tail, cw, *, pages_per_step):
    w_ab, pe8, w1b, w2x = cw
    ab = cmp_ab_paged(pool, page_table, w_ab, pages_per_step=pages_per_step)
    one = jnp.zeros((1, 1), jnp.int32)
    ab_tail = cmp_ab(tail.reshape(1, -1, SUB_W), one, w_ab, pages_per_step=1)
    ab_tail = ab_tail.reshape(tail.shape[0], tail.shape[1], -1)
    return cmp_fin(ab, ab_tail, pe8, w1b, w2x)


def _position_minor(a):
    n, p = a.shape[0], a.shape[1]
    return jnp.transpose(a, (0, 2, 3, 1)).reshape(n, SLAB, p)


def _position_major(a_t):
    n, _, p = a_t.shape
    return jnp.transpose(a_t.reshape(n, N_KV, HEAD_DIM, p), (0, 3, 1, 2))


def kernel(x_prompt, x_sample, cache_cmp_k, cache_cmp_v, cache_sel_k, cache_sel_v, cache_win_k,
           cache_win_v, page_table, ffn1_w_in, ffn1_w_out, ln1_g, ln1_b, w_in, cmp_pe_k, cmp_w1_k,
           cmp_w2_k, cmp_pe_v, cmp_w1_v, cmp_w2_v, gmlp_ln_g, gmlp_ln_b, spatial_w, spatial_b,
           w_branch_a, w_branch_b, w_out, ln2_g, ln2_b, ffn2_w_in, ffn2_w_out, ln3_g, ln3_b):
    nb, t, d = x_prompt.shape
    db, n_new, _ = x_sample.shape
    assert n_new == 1 and t % SEL_TILE == 0
    n_pages = page_table.shape[1]
    past = n_pages * PAGE_SIZE
    n_phys = cache_cmp_k.shape[0]
    m = nb * t

    f1_in, f1_out = ffn1_w_in.astype(BF16), ffn1_w_out.astype(BF16)
    f2_in, f2_out = ffn2_w_in.astype(BF16), ffn2_w_out.astype(BF16)
    proj_w = _split_w_in(w_in)
    wa = _expand_branch_a(w_branch_a)
    wb = w_branch_b.astype(BF16)
    wo = w_out.astype(BF16)
    cw_k = _cmp_weights(cmp_pe_k, cmp_w1_k, cmp_w2_k)
    cw_v = _cmp_weights(cmp_pe_v, cmp_w1_v, cmp_w2_v)

    h = ffn_ln(x_prompt.reshape(m, d), f1_in, f1_out, ln1_g, ln1_b, tm=1024)
    (qt, kct, vct, kst, vst, kwt, vwt, kcb, vcb, ksb, kwb, vstb, vwtb, gtt, uv,
     sgab) = mixer_proj_t(h, *proj_w, nb=nb, tm=512)
    n_cmp = t // CMP_STRIDE
    kcc = _compress_rows(kcb.reshape(nb, n_cmp, SUB_W), cw_k)
    vcc = _compress_rows(vcb.reshape(nb, n_cmp, SUB_W), cw_v)
    imp_t = _importance_matrix(n_cmp, t // SEL_BLOCK).T
    o_a = nsa_prompt(qt, gtt, kcc, jnp.swapaxes(vcc, 1, 2), ksb.reshape(nb, t, SLAB), vstb,
                     kwb.reshape(nb, t, SLAB), vwtb, imp_t)
    sb_tile = jnp.repeat(spatial_b.T, GROUP_CH, axis=1)
    h2 = merge_ln(h, o_a.reshape(m, Q_EXP), uv, sgab, wa, wb, wo, ln2_g, ln2_b, gmlp_ln_g,
                  gmlp_ln_b, spatial_w, sb_tile, tm=512, chunked=True)
    y_prompt = ffn_ln(h2, f2_in, f2_out, ln3_g, ln3_b, tm=1024).reshape(nb, t, d)
    wkeep = min(WINDOW, t)
    p_outs = tuple(_position_major(a) for a in
                   (kct, vct, kst, vst, kwt[:, :, t - wkeep:], vwt[:, :, t - wkeep:]))

    hs = ffn_ln(x_sample.reshape(db, d), f1_in, f1_out, ln1_g, ln1_b, tm=db)
    (qs, kcs, vcs, kss, vss, kws, vws, _, gts, uvs, sgabs) = mixer_proj(hs, *proj_w, tm=db)
    t_pad = -(-(past + n_new) // SEL_BLOCK) * SEL_BLOCK
    n_tail = (t_pad - past) // CMP_STRIDE
    tail_rows = 16

    def tail_of(new):
        flat = jnp.pad(new, ((0, 0), (0, tail_rows * SUB_W - SLAB)))
        return flat.reshape(db, tail_rows, SUB_W)

    assert n_tail <= tail_rows
    pps = min(64, n_pages)
    kcc_s = _compress_paged(_position_minor(cache_cmp_k), page_table, tail_of(kcs), cw_k,
                            pages_per_step=pps)
    vcc_s = _compress_paged(_position_minor(cache_cmp_v), page_table, tail_of(vcs), cw_v,
                            pages_per_step=pps)
    n_sel_s = t_pad // SEL_BLOCK
    n_cand = -(-n_sel_s // 128) * 128
    imp_s = _importance_matrix(kcc_s.shape[1], n_cand)
    q8 = qs.reshape(db, N_HEADS, SLAB)
    o_cmp_s, idx_s = nsa_sample_cmp(q8, kcc_s, vcc_s, imp_s, qpos=past, n_sel=n_sel_s)
    idx_flat = idx_s[:, :N_KV, :SEL_TOP].reshape(db, N_KV * SEL_TOP)
    gt8 = jnp.pad(gts[:, :GATE_W].reshape(db, N_HEADS, 3), ((0, 0), (0, 0), (0, 125)))
    row3 = lambda a: a.reshape(db, 1, SLAB)
    win3 = lambda a: a.reshape(db, -1, SLAB)
    o_as = nsa_sample_sel(page_table, idx_flat, _position_minor(cache_sel_k),
                          _position_minor(cache_sel_v),
                          q8, gt8, o_cmp_s, row3(kss), row3(vss), win3(cache_win_k),
                          win3(cache_win_v), row3(kws), row3(vws), qpos=past)
    sw_row = jnp.repeat(spatial_w[:, 0, 0], GROUP_CH).reshape(1, D_B)
    sb_row = jnp.repeat(spatial_b[:, 0], GROUP_CH).reshape(1, D_B)
    hs2, vn_s = merge_ln(hs, o_as.reshape(db, Q_EXP), uvs, sgabs, wa, wb, wo, ln2_g, ln2_b,
                         gmlp_ln_g, gmlp_ln_b, sw_row, sb_row, tm=db, chunked=False)
    y_sample = ffn_ln(hs2, f2_in, f2_out, ln3_g, ln3_b, tm=db).reshape(db, n_new, d)
    kv4s = lambda a: a.reshape(db, n_new, N_KV, HEAD_DIM)
    wbuf = cache_win_k.shape[1]
    s_win_k = jnp.concatenate([cache_win_k, kv4s(kws)], axis=1)[:, -wbuf:]
    s_win_v = jnp.concatenate([cache_win_v, kv4s(vws)], axis=1)[:, -wbuf:]

    return (y_prompt, y_sample) + p_outs + (kv4s(kcs), kv4s(vcs), kv4s(kss), kv4s(vss),
                                            s_win_k, s_win_v, vn_s.reshape(db, n_new, D_B))
```

```python
import functools

import numpy as np
import jax
import jax.numpy as jnp
from jax import lax
from jax.experimental import pallas as pl
from jax.experimental.pallas import tpu as pltpu

F32 = jnp.float32
BF16 = jnp.bfloat16

D_MODEL = 1024
N_HEADS = 8
HEAD_DIM = 64
N_KV = 2
GROUP = N_HEADS // N_KV
PAGE_SIZE = 128
CMP_LEN = 32
CMP_STRIDE = 16
CMP_HIDDEN = 128
SEL_BLOCK = 64
CMP_RATIO = SEL_BLOCK // CMP_STRIDE
SEL_TOP = 16
WINDOW = 512
Q_BLOCK = 128
FORCE_SCORE = 1e4
D_B = 512
N_GROUPS_B = 4
GROUP_CH = D_B // N_GROUPS_B
CHUNK = 128
D_FF = 2816
DEPTH = 1
ALPHA = (2.0 * DEPTH) ** 0.25
LN_EPS = 1e-5
NEG_INF = -1e30
REMOVED = -3e38
LOG2E = float(np.log2(np.e))
Q_W = N_HEADS * HEAD_DIM
KV_W = N_KV * HEAD_DIM
GATE_W = N_HEADS * 3
SLAB = N_KV * HEAD_DIM
Q_EXP = N_HEADS * SLAB
SUB_W = CMP_STRIDE * SLAB
VMEM_LIMIT = 40 * 1024 * 1024
SEL_TILE = 1024
CMP_PAGE_GROUPS = 4


def _params(sem, vmem=VMEM_LIMIT, flags=None):
    return pltpu.CompilerParams(dimension_semantics=sem, vmem_limit_bytes=vmem, flags=flags)


def _layer_norm(x, g, b):
    mu = jnp.mean(x, axis=-1, keepdims=True)
    xc = x - mu
    var = jnp.mean(xc * xc, axis=-1, keepdims=True)
    return xc * lax.rsqrt(var + LN_EPS) * g + b


def _dot(a, b):
    return jnp.dot(a, b, preferred_element_type=F32)


def _dot_nt(a, b):
    return lax.dot_general(a, b, (((1,), (1,)), ((), ())), preferred_element_type=F32)


def _split3(x):
    hi = x.astype(BF16)
    r1 = x - hi.astype(F32)
    mid = r1.astype(BF16)
    lo = (r1 - mid.astype(F32)).astype(BF16)
    return hi, mid, lo


def _dot_exact_rhs(x, a_bf16):
    hi, mid, lo = _split3(x)
    return _dot(hi, a_bf16) + _dot(mid, a_bf16) + _dot(lo, a_bf16)


def _ffn_ln_kernel(x_ref, wg_ref, wu_ref, wo_ref, g_ref, b_ref, o_ref, xb_ref, acc_ref, *, n_f):
    f = pl.program_id(1)

    @pl.when(f == 0)
    def _():
        xb_ref[...] = x_ref[...].astype(BF16)
        acc_ref[...] = jnp.zeros_like(acc_ref)

    xb = xb_ref[...]
    gate = _dot(xb, wg_ref[...])
    up = _dot(xb, wu_ref[...])
    hid = (gate * jax.nn.sigmoid(gate)) * up
    acc_ref[...] += _dot(hid.astype(BF16), wo_ref[...])

    @pl.when(f == n_f - 1)
    def _():
        y = ALPHA * x_ref[...] + 0.5 * acc_ref[...]
        o_ref[...] = _layer_norm(y, g_ref[...], b_ref[...])


def ffn_ln(x, w_in_b, w_out_b, g, b, *, tm, tf=256):
    m, d = x.shape
    d_ff = w_out_b.shape[0]
    n_f = d_ff // tf
    return pl.pallas_call(
        functools.partial(_ffn_ln_kernel, n_f=n_f),
        grid=(m // tm, n_f),
        in_specs=[
            pl.BlockSpec((tm, d), lambda i, f: (i, 0)),
            pl.BlockSpec((d, tf), lambda i, f: (0, f)),
            pl.BlockSpec((d, tf), lambda i, f: (0, f + n_f)),
            pl.BlockSpec((tf, d), lambda i, f: (f, 0)),
            pl.BlockSpec((1, d), lambda i, f: (0, 0)),
            pl.BlockSpec((1, d), lambda i, f: (0, 0)),
        ],
        out_specs=pl.BlockSpec((tm, d), lambda i, f: (i, 0)),
        out_shape=jax.ShapeDtypeStruct((m, d), F32),
        scratch_shapes=[pltpu.VMEM((tm, d), BF16), pltpu.VMEM((tm, d), F32)],
        compiler_params=_params(("parallel", "arbitrary")),
        name="ffn_ln",
    )(x, w_in_b, w_in_b, w_out_b, g.reshape(1, d), b.reshape(1, d))


def _mixer_proj_kernel(h_ref, wq_ref, wkv_ref, wgt_ref, wuv_ref, wgab_ref,
                       q_ref, kc_ref, vc_ref, ks_ref, vs_ref, kw_ref, vw_ref,
                       kvb_ref, gt_ref, uv_ref, sgab_ref):
    hb = h_ref[...].astype(BF16)
    q_ref[...] = (_dot(hb, wq_ref[...]) * (HEAD_DIM ** -0.5)).astype(BF16)
    kv = _dot(hb, wkv_ref[...])
    for j, ref in enumerate((kc_ref, vc_ref, ks_ref, vs_ref, kw_ref, vw_ref)):
        ref[...] = kv[:, j * SLAB:(j + 1) * SLAB]
    kvb_ref[...] = kv.astype(BF16)
    gt_ref[...] = jax.nn.sigmoid(_dot(hb, wgt_ref[...]))
    uv_ref[...] = jax.nn.gelu(_dot(hb, wuv_ref[...]))
    sgab_ref[...] = jax.nn.sigmoid(_dot(hb, wgab_ref[...])).astype(BF16)


def mixer_proj(h, wq, wkv, wgt, wuv, wgab, *, tm):
    m, d = h.shape
    row = lambda n: pl.BlockSpec((tm, n), lambda i: (i, 0))
    full = lambda w: pl.BlockSpec(w.shape, lambda i: (0, 0))
    out_shape = (
        [jax.ShapeDtypeStruct((m, Q_EXP), BF16)]
        + [jax.ShapeDtypeStruct((m, SLAB), F32)] * 6
        + [jax.ShapeDtypeStruct((m, 6 * SLAB), BF16),
           jax.ShapeDtypeStruct((m, 128), F32),
           jax.ShapeDtypeStruct((m, 2 * D_B), F32),
           jax.ShapeDtypeStruct((m, 2 * D_MODEL), BF16)])
    out_specs = ([row(Q_EXP)] + [row(SLAB)] * 6
                 + [row(6 * SLAB), row(128), row(2 * D_B), row(2 * D_MODEL)])
    return pl.pallas_call(
        _mixer_proj_kernel,
        grid=(m // tm,),
        in_specs=[row(d), full(wq), full(wkv), full(wgt), full(wuv), full(wgab)],
        out_specs=out_specs,
        out_shape=out_shape,
        compiler_params=_params(("parallel",)),
        name="mixer_proj",
    )(h, wq, wkv, wgt, wuv, wgab)


def _mixer_proj_t_kernel(h_ref, wq_ref, wkv_ref, wgt_ref, wuv_ref, wgab_ref,
                         qt_ref, kct_ref, vct_ref, kst_ref, vst_ref, kwt_ref, vwt_ref,
                         kcb_ref, vcb_ref, ksb_ref, kwb_ref, vstb_ref, vwtb_ref,
                         gtt_ref, uv_ref, sgab_ref):
    hb = h_ref[...].astype(BF16)
    q = _dot(hb, wq_ref[...]) * (HEAD_DIM ** -0.5)
    for h in range(N_HEADS):
        qt_ref[h * SLAB:(h + 1) * SLAB, :] = q[:, h * SLAB:(h + 1) * SLAB].T.astype(BF16)
    kv = _dot(hb, wkv_ref[...])
    slabs = [kv[:, j * SLAB:(j + 1) * SLAB] for j in range(6)]
    for slab, ref in zip(slabs, (kct_ref, vct_ref, kst_ref, vst_ref, kwt_ref, vwt_ref)):
        ref[...] = slab.T
    kcb_ref[...] = slabs[0].astype(BF16)
    vcb_ref[...] = slabs[1].astype(BF16)
    ksb_ref[...] = slabs[2].astype(BF16)
    kwb_ref[...] = slabs[4].astype(BF16)
    vstb_ref[...] = slabs[3].T.astype(BF16)
    vwtb_ref[...] = slabs[5].T.astype(BF16)
    gtt_ref[...] = jax.nn.sigmoid(_dot(hb, wgt_ref[...])).T
    uv_ref[...] = jax.nn.gelu(_dot(hb, wuv_ref[...]))
    sgab_ref[...] = jax.nn.sigmoid(_dot(hb, wgab_ref[...])).astype(BF16)


def mixer_proj_t(h, wq, wkv, wgt, wuv, wgab, *, nb, tm):
    m, d = h.shape
    t = m // nb
    per_b = t // tm
    row = lambda n: pl.BlockSpec((tm, n), lambda i: (i, 0))
    col = lambda n: pl.BlockSpec((None, n, tm), lambda i: (i // per_b, 0, i % per_b))
    full = lambda w: pl.BlockSpec(w.shape, lambda i: (0, 0))
    sds = jax.ShapeDtypeStruct
    out_shape = ([sds((nb, Q_EXP, t), BF16)] + [sds((nb, SLAB, t), F32)] * 6
                 + [sds((m, SLAB), BF16)] * 4 + [sds((nb, SLAB, t), BF16)] * 2
                 + [sds((nb, 128, t), F32), sds((m, 2 * D_B), F32), sds((m, 2 * D_MODEL), BF16)])
    out_specs = ([col(Q_EXP)] + [col(SLAB)] * 6 + [row(SLAB)] * 4 + [col(SLAB)] * 2
                 + [col(128), row(2 * D_B), row(2 * D_MODEL)])
    return pl.pallas_call(
        _mixer_proj_t_kernel,
        grid=(m // tm,),
        in_specs=[row(d), full(wq), full(wkv), full(wgt), full(wuv), full(wgab)],
        out_specs=out_specs,
        out_shape=out_shape,
        compiler_params=_params(("parallel",)),
        name="mixer_proj_t",
    )(h, wq, wkv, wgt, wuv, wgab)


def _cmp_ab_kernel(pt_ref, *refs, n_pages):
    del pt_ref
    page_refs = refs[:n_pages]
    w_ref = refs[n_pages]
    o_ref = refs[n_pages + 1]
    if n_pages == 1:
        x = page_refs[0][...]
    else:
        x = jnp.concatenate([r[...] for r in page_refs], axis=0)
    o_ref[...] = _dot(x.astype(BF16), w_ref[...])


def cmp_ab(pool, page_table, w_ab, *, pages_per_step):
    nb, n_pages = page_table.shape
    rows = pool.shape[1]
    pp = pages_per_step
    n_steps = n_pages // pp

    def page_spec(k):
        return pl.BlockSpec((None, rows, SUB_W), lambda b, s, pt: (pt[b, s * pp + k], 0, 0))

    grid_spec = pltpu.PrefetchScalarGridSpec(
        num_scalar_prefetch=1,
        grid=(nb, n_steps),
        in_specs=[page_spec(k) for k in range(pp)]
        + [pl.BlockSpec(w_ab.shape, lambda b, s, pt: (0, 0))],
        out_specs=pl.BlockSpec((None, pp * rows, w_ab.shape[1]), lambda b, s, pt: (b, s, 0)),
    )
    return pl.pallas_call(
        functools.partial(_cmp_ab_kernel, n_pages=pp),
        grid_spec=grid_spec,
        out_shape=jax.ShapeDtypeStruct((nb, n_pages * rows, w_ab.shape[1]), F32),
        compiler_params=_params(("parallel", "arbitrary")),
        name="cmp_ab",
    )(page_table, *([pool] * pp), w_ab)


def _cmp_ab_paged_kernel(pt_ref, *refs, n_pages):
    del pt_ref
    page_refs = refs[:n_pages]
    w_ref, o_ref, rows_ref = refs[n_pages:]
    per_group = n_pages // CMP_PAGE_GROUPS
    n_sub = per_group * PAGE_SIZE // CMP_STRIDE
    for grp in range(CMP_PAGE_GROUPS):
        base = grp * per_group * PAGE_SIZE
        for k in range(per_group):
            rows_ref[base + k * PAGE_SIZE:base + (k + 1) * PAGE_SIZE, :] = (
                page_refs[grp * per_group + k][...].T)
        x = jnp.concatenate([rows_ref[pl.ds(base + p, n_sub, stride=CMP_STRIDE), :].astype(BF16)
                             for p in range(CMP_STRIDE)], axis=1)
        o_ref[grp * n_sub:(grp + 1) * n_sub, :] = _dot(x, w_ref[...])


def cmp_ab_paged(pool, page_table, w_ab, *, pages_per_step):
    nb, n_pages = page_table.shape
    pp = pages_per_step
    sub_per_page = PAGE_SIZE // CMP_STRIDE

    n_phys = pool.shape[0]

    def page_spec(k):
        return pl.BlockSpec((None, SLAB, PAGE_SIZE),
                            lambda b, s, pt: (jnp.clip(pt[b, s * pp + k], 0, n_phys - 1), 0, 0))

    grid_spec = pltpu.PrefetchScalarGridSpec(
        num_scalar_prefetch=1,
        grid=(nb, n_pages // pp),
        in_specs=[page_spec(k) for k in range(pp)]
        + [pl.BlockSpec(w_ab.shape, lambda b, s, pt: (0, 0))],
        out_specs=pl.BlockSpec((None, pp * sub_per_page, w_ab.shape[1]), lambda b, s, pt: (b, s, 0)),
        scratch_shapes=[pltpu.VMEM((pp * PAGE_SIZE, SLAB), F32)],
    )
    return pl.pallas_call(
        functools.partial(_cmp_ab_paged_kernel, n_pages=pp),
        grid_spec=grid_spec,
        out_shape=jax.ShapeDtypeStruct((nb, n_pages * sub_per_page, w_ab.shape[1]), F32),
        compiler_params=_params(("parallel", "arbitrary")),
        name="cmp_ab_paged",
    )(page_table, *([pool] * pp), w_ab)


def _cmp_fin_kernel(*refs, has_tail):
    if has_tail:
        ab_ref, tail_ref, pe_ref, w1_ref, w2_ref, o_ref = refs
    else:
        ab_ref, pe_ref, w1_ref, w2_ref, o_ref = refs
    hw = 2 * CMP_HIDDEN
    n = ab_ref.shape[0]
    pe_h = _dot(pe_ref[...], w1_ref[...])[0:1, :]
    bias = jnp.concatenate([pe_h, pe_h], axis=1)
    w2 = w2_ref[...]
    first = ab_ref[:, 0:hw]
    second = pltpu.roll(ab_ref[:, hw:2 * hw], n - 1, 0)
    if has_tail:
        t_first = tail_ref[:, 0:hw]
        t_second = tail_ref[:, hw:2 * hw]
        nt = tail_ref.shape[0]
        is_last = lax.broadcasted_iota(jnp.int32, (n, 1), 0) == n - 1
        second = jnp.where(is_last, t_second[0:1, :], second)
        t_hid = t_first + pltpu.roll(t_second, nt - 1, 0) + bias
        o_ref[n:n + nt, :] = _dot(jax.nn.gelu(t_hid).astype(BF16), w2).astype(BF16)
        n_out = o_ref.shape[0]
        if n_out > n + nt:
            o_ref[n + nt:n_out, :] = jnp.zeros((n_out - n - nt, SLAB), BF16)
    hid = first + second + bias
    o_ref[0:n, :] = _dot(jax.nn.gelu(hid).astype(BF16), w2).astype(BF16)


def cmp_fin(ab, tail, pe8, w1b, w2x):
    nb, n, w = ab.shape
    has_tail = tail is not None
    nt = tail.shape[1] if has_tail else 0
    full = lambda a: pl.BlockSpec(a.shape, lambda b: (0, 0))
    in_specs = [pl.BlockSpec((None, n, w), lambda b: (b, 0, 0))]
    args = [ab]
    if has_tail:
        in_specs.append(pl.BlockSpec((None, nt, w), lambda b: (b, 0, 0)))
        args.append(tail)
    in_specs += [full(pe8), full(w1b), full(w2x)]
    args += [pe8, w1b, w2x]
    n_out = -(-(n + nt) // 128) * 128 if has_tail else n
    return pl.pallas_call(
        functools.partial(_cmp_fin_kernel, has_tail=has_tail),
        grid=(nb,),
        in_specs=in_specs,
        out_specs=pl.BlockSpec((None, n_out, SLAB), lambda b: (b, 0, 0)),
        out_shape=jax.ShapeDtypeStruct((nb, n_out, SLAB), BF16),
        compiler_params=_params(("parallel",)),
        name="cmp_fin",
    )(*args)


def _slope_of_head(h):
    return 2.0 ** (-(h + 1))


def _slope_column():
    head = lax.broadcasted_iota(jnp.int32, (N_HEADS, 1), 0)
    slope = jnp.zeros((N_HEADS, 1), F32)
    for h in range(N_HEADS):
        slope = jnp.where(head == h, _slope_of_head(h), slope)
    return slope


def _top_k_mask_t(score_t, k):
    n = score_t.shape[0]
    cand_idx = lax.broadcasted_iota(jnp.int32, score_t.shape, 0).astype(F32)

    def body(_, carry):
        s, sel = carry
        m = jnp.max(s, axis=0, keepdims=True)
        first = jnp.min(jnp.where(s == m, cand_idx, float(n)), axis=0, keepdims=True)
        pick = cand_idx == first
        return jnp.where(pick, REMOVED, s), jnp.where(pick, 1.0, sel)

    _, sel = lax.fori_loop(0, k, body, (score_t, jnp.zeros_like(score_t)), unroll=True)
    return sel


def _softmax_masked(s, mask):
    s = jnp.where(mask, s, NEG_INF)
    m = jnp.max(s, axis=-1, keepdims=True)
    e = jnp.exp(s - m)
    return jnp.where(mask, e / jnp.sum(e, axis=-1, keepdims=True), 0.0)


def _softmax_masked_t(s, mask):
    s = jnp.where(mask, s, NEG_INF)
    m = jnp.max(s, axis=0, keepdims=True)
    e = jnp.exp2(s - m)
    return jnp.where(mask, e * (1.0 / jnp.sum(e, axis=0, keepdims=True)), 0.0)


def _nsa_prompt_kernel(qt_ref, gtt_ref, kc_ref, vct_ref, ks_ref, vst_ref, kw_ref, vwt_ref, impt_ref,
                       o_ref, selt_ref, bias_c_ref, bias_w_ref, bias_s_ref):
    qb = pl.program_id(1)
    s0 = qb * Q_BLOCK
    nq = Q_BLOCK
    hq = N_HEADS * nq
    nc = kc_ref.shape[0]
    n_sel = ks_ref.shape[0] // SEL_BLOCK
    tk = SEL_TILE
    assert HEAD_DIM == 64 and SEL_BLOCK == 64 and Q_BLOCK == 128

    lane = lax.broadcasted_iota(jnp.int32, (1, hq), 1)
    q_loc = lane & (nq - 1)
    qpos_f = (s0 + q_loc).astype(F32)
    qpos_q = s0 + lax.broadcasted_iota(jnp.int32, (1, nq), 1)
    row_half = lax.broadcasted_iota(jnp.int32, (SLAB, 1), 0) >> 6
    slope = jnp.zeros((1, hq), F32)
    for h in range(N_HEADS):
        slope = jnp.where((lane >> 7) == h, _slope_of_head(h) * LOG2E, slope)
    n_win = WINDOW + nq
    k_loc = lax.broadcasted_iota(jnp.int32, (tk, 1), 0)

    @pl.when(qb == 0)
    def _():
        q_loc_f = q_loc.astype(F32)
        c_end = lax.broadcasted_iota(jnp.int32, (nc, 1), 0) * CMP_STRIDE + (CMP_LEN - 1)
        bias_c_ref[...] = slope * (q_loc_f - c_end.astype(F32))
        w_loc = lax.broadcasted_iota(jnp.int32, (n_win, 1), 0)
        bias_w_ref[...] = slope * (q_loc_f - w_loc.astype(F32))
        bias_s_ref[...] = slope * k_loc.astype(F32)

    j_idx = lax.broadcasted_iota(jnp.int32, (n_sel, 1), 0)
    cur = qpos_q >> 6
    forced = (j_idx == 0) | (j_idx == cur) | (j_idx == cur - 1)
    valid = (j_idx * SEL_BLOCK) <= qpos_q

    w0 = pl.multiple_of(jnp.maximum(s0 - WINDOW, 0), Q_BLOCK)
    kw = kw_ref[pl.ds(w0, n_win), :]
    vwt = vwt_ref[:, pl.ds(w0, n_win)]

    assert tk % nq == 0
    n_full = s0 // tk
    blocks_per_tile = tk // SEL_BLOCK

    qt = jnp.concatenate([qt_ref[h * SLAB:(h + 1) * SLAB, :] for h in range(N_HEADS)],
                         axis=1)

    bias_c = bias_c_ref[...]
    mask_c = bias_c >= slope * (-s0).astype(F32)
    p = _softmax_masked_t(_dot(kc_ref[...], qt) - bias_c, mask_c)
    o_cmp = _dot(vct_ref[...], p.astype(BF16))

    imp_t = impt_ref[...]
    scores = []
    for g in range(N_KV):
        p_sum = p[:, g * GROUP * nq:(g * GROUP + 1) * nq]
        for r in range(1, GROUP):
            p_sum = p_sum + p[:, (g * GROUP + r) * nq:(g * GROUP + r + 1) * nq]
        hi, mid, lo = _split3(p_sum)
        p_slc = _dot(imp_t, hi) + _dot(imp_t, mid) + _dot(imp_t, lo)
        scores.append(jnp.where(valid, jnp.where(forced, FORCE_SCORE, p_slc), NEG_INF))
    sel = _top_k_mask_t(jnp.concatenate(scores, axis=1), min(SEL_TOP, n_sel))
    for g in range(N_KV):
        selt_ref[g] = sel[:, g * nq:(g + 1) * nq]


    def sel_tile(t, carry, diagonal):
        m_old, l_old, acc_old = carry
        kt = pl.multiple_of(t * tk, tk)
        k = ks_ref[pl.ds(kt, tk), :]
        vt = vst_ref[:, pl.ds(kt, tk)]
        s_rel = _dot(k, qt) + bias_s_ref[...]
        shift = slope * (kt.astype(F32) - qpos_f)
        cols = []
        for g in range(N_KV):
            picked = jnp.concatenate(
                [jnp.broadcast_to(selt_ref[g, pl.ds(t * blocks_per_tile + jj, 1), :],
                                  (SEL_BLOCK, nq))
                 for jj in range(blocks_per_tile)], axis=0) > 0.5
            if diagonal:
                picked = picked & ((kt + k_loc) <= qpos_q)
            for r in range(GROUP):
                h = g * GROUP + r
                cols.append(jnp.where(picked, s_rel[:, h * nq:(h + 1) * nq], NEG_INF))
        s_rel = jnp.concatenate(cols, axis=1)
        m_new = jnp.maximum(m_old, jnp.max(s_rel, axis=0, keepdims=True) + shift)
        alpha = jnp.exp2(m_old - m_new)
        p_t = jnp.exp2(s_rel - (m_new - shift))
        l_new = alpha * l_old + jnp.sum(p_t, axis=0, keepdims=True)
        acc_new = alpha * acc_old + _dot(vt, p_t.astype(BF16))
        return m_new, l_new, acc_new

    init = (jnp.full((1, hq), NEG_INF, F32), jnp.zeros((1, hq), F32), jnp.zeros((SLAB, hq), F32))
    state = lax.fori_loop(0, n_full, functools.partial(sel_tile, diagonal=False), init)
    _, l_sel, acc_sel = sel_tile(n_full, state, diagonal=True)
    o_sel = acc_sel * (1.0 / l_sel)

    bias_w = bias_w_ref[...]
    lead = (s0 - w0).astype(F32)
    mask_w = (bias_w >= slope * (-lead)) & (bias_w < slope * (WINDOW - lead))
    o_win = _dot(vwt, _softmax_masked_t(_dot(kw, qt) - bias_w, mask_w).astype(BF16))

    for h in range(N_HEADS):
        cols = slice(h * nq, (h + 1) * nq)
        o = (gtt_ref[3 * h:3 * h + 1, :] * o_cmp[:, cols]
             + gtt_ref[3 * h + 1:3 * h + 2, :] * o_sel[:, cols]
             + gtt_ref[3 * h + 2:3 * h + 3, :] * o_win[:, cols])
        o_ref[:, h * SLAB:(h + 1) * SLAB] = jnp.where(row_half == h // GROUP, o, 0.0).T.astype(BF16)


def nsa_prompt(qt, gtt, kc, vct, ks, vst, kw, vwt, imp_t):
    nb, _, t = qt.shape
    nc = kc.shape[1]
    rows = pl.BlockSpec((None, t, SLAB), lambda b, i: (b, 0, 0))
    cols = pl.BlockSpec((None, SLAB, t), lambda b, i: (b, 0, 0))
    return pl.pallas_call(
        _nsa_prompt_kernel,
        grid=(nb, t // Q_BLOCK),
        in_specs=[
            pl.BlockSpec((None, Q_EXP, Q_BLOCK), lambda b, i: (b, 0, i)),
            pl.BlockSpec((None, 128, Q_BLOCK), lambda b, i: (b, 0, i)),
            pl.BlockSpec((None, nc, SLAB), lambda b, i: (b, 0, 0)),
            pl.BlockSpec((None, SLAB, nc), lambda b, i: (b, 0, 0)),
            rows, cols, rows, cols,
            pl.BlockSpec(imp_t.shape, lambda b, i: (0, 0)),
        ],
        out_specs=pl.BlockSpec((None, Q_BLOCK, Q_EXP), lambda b, i: (b, i, 0)),
        out_shape=jax.ShapeDtypeStruct((nb, t, Q_EXP), BF16),
        scratch_shapes=[pltpu.VMEM((N_KV, t // SEL_BLOCK, Q_BLOCK), F32),
                        pltpu.VMEM((nc, N_HEADS * Q_BLOCK), F32),
                        pltpu.VMEM((WINDOW + Q_BLOCK, N_HEADS * Q_BLOCK), F32),
                        pltpu.VMEM((SEL_TILE, N_HEADS * Q_BLOCK), F32)],
        compiler_params=_params(("parallel", "arbitrary")),
        name="nsa_prompt",
    )(qt, gtt, kc, vct, ks, vst, kw, vwt, imp_t)


def _nsa_sample_cmp_kernel(q_ref, kc_ref, vc_ref, imp_ref, oc_ref, idx_ref, *, qpos, n_sel):
    nc = kc_ref.shape[0]
    n_cand = imp_ref.shape[1]
    q = q_ref[...]
    head = lax.broadcasted_iota(jnp.int32, (N_HEADS, 1), 0)
    slope = _slope_column()
    c_end = lax.broadcasted_iota(jnp.int32, (1, nc), 1) * CMP_STRIDE + (CMP_LEN - 1)
    dist = qpos - c_end
    mask = dist >= 0
    s = _dot_nt(q, kc_ref[...]) - slope * dist.astype(F32)
    p = _softmax_masked(s, mask)
    oc_ref[...] = _dot(p.astype(BF16), vc_ref[...])

    p_sum = jnp.zeros_like(p)
    for g in range(N_KV):
        grp = jnp.sum(p[g * GROUP:(g + 1) * GROUP], axis=0, keepdims=True)
        p_sum = jnp.where(head == g, grp, p_sum)
    p_slc = _dot_exact_rhs(p_sum, imp_ref[...])

    j_idx = lax.broadcasted_iota(jnp.int32, (1, n_cand), 1)
    cur = qpos // SEL_BLOCK
    forced = (j_idx == 0) | (j_idx == cur) | (j_idx == cur - 1)
    valid = (j_idx * SEL_BLOCK) <= qpos
    score = jnp.where(valid, jnp.where(forced, FORCE_SCORE, p_slc), NEG_INF)
    score = jnp.where(j_idx < n_sel, score, REMOVED)

    cand = lax.broadcasted_iota(jnp.int32, score.shape, 1).astype(F32)
    out_lane = lax.broadcasted_iota(jnp.int32, (N_HEADS, 128), 1)

    def body(k, carry):
        sc, out = carry
        m = jnp.max(sc, axis=1, keepdims=True)
        first = jnp.min(jnp.where(sc == m, cand, float(n_cand)), axis=1, keepdims=True)
        return (jnp.where(cand == first, REMOVED, sc),
                jnp.where(out_lane == k, first.astype(jnp.int32), out))

    _, out = lax.fori_loop(0, SEL_TOP, body, (score, jnp.zeros((N_HEADS, 128), jnp.int32)),
                           unroll=True)
    idx_ref[...] = out


def nsa_sample_cmp(q8, kc, vc, imp, *, qpos, n_sel):
    nb = q8.shape[0]
    nc = kc.shape[1]
    return pl.pallas_call(
        functools.partial(_nsa_sample_cmp_kernel, qpos=qpos, n_sel=n_sel),
        grid=(nb,),
        in_specs=[
            pl.BlockSpec((None, N_HEADS, SLAB), lambda b: (b, 0, 0)),
            pl.BlockSpec((None, nc, SLAB), lambda b: (b, 0, 0)),
            pl.BlockSpec((None, nc, SLAB), lambda b: (b, 0, 0)),
            pl.BlockSpec(imp.shape, lambda b: (0, 0)),
        ],
        out_specs=[pl.BlockSpec((None, N_HEADS, SLAB), lambda b: (b, 0, 0)),
                   pl.BlockSpec((None, N_HEADS, 128), lambda b: (b, 0, 0))],
        out_shape=[jax.ShapeDtypeStruct((nb, N_HEADS, SLAB), F32),
                   jax.ShapeDtypeStruct((nb, N_HEADS, 128), jnp.int32)],
        compiler_params=_params(("parallel",)),
        name="nsa_sample_cmp",
    )(q8, kc, vc, imp)


def _nsa_sample_sel_kernel(pt_ref, idx_ref, *refs, qpos, n_past_blocks):
    del pt_ref
    n_pg = N_KV * SEL_TOP
    k_pages = refs[:n_pg]
    v_pages = refs[n_pg:2 * n_pg]
    (q_ref, gt_ref, oc_ref, kn_ref, vn_ref, wk_ref, wv_ref, wkn_ref, wvn_ref,
     o_ref, kr_ref, vr_ref) = refs[2 * n_pg:]
    b = pl.program_id(0)
    q = q_ref[...]
    head = lax.broadcasted_iota(jnp.int32, (N_HEADS, 1), 0)
    slope = _slope_column()
    grp_of_row = head >> 2
    assert GROUP == 4 and HEAD_DIM == 64 and SEL_BLOCK == 64
    lane_half = lax.broadcasted_iota(jnp.int32, (1, SLAB), 1) >> 6
    blocks_per_page = PAGE_SIZE // SEL_BLOCK
    n_keys = SEL_TOP * SEL_BLOCK
    key_slot = lax.broadcasted_iota(jnp.int32, (1, n_keys), 1) >> 6
    key_off = lax.broadcasted_iota(jnp.int32, (1, n_keys), 1) & (SEL_BLOCK - 1)
    qf = q.astype(F32)

    for i in range(n_pg):
        kr_ref[i * PAGE_SIZE:(i + 1) * PAGE_SIZE, :] = k_pages[i][...].T
        vr_ref[i * PAGE_SIZE:(i + 1) * PAGE_SIZE, :] = v_pages[i][...].T

    s_new = jnp.sum(qf * kn_ref[...], axis=1, keepdims=True)

    o_sel = jnp.zeros((N_HEADS, SLAB), F32)
    for g in range(N_KV):
        ks, vs = [], []
        start = jnp.zeros((1, n_keys), jnp.int32)
        for k in range(SEL_TOP):
            i = g * SEL_TOP + k
            j = idx_ref[b, i]
            off = pl.multiple_of(i * PAGE_SIZE + (j % blocks_per_page) * SEL_BLOCK, SEL_BLOCK)
            ks.append(kr_ref[pl.ds(off, SEL_BLOCK), :])
            vs.append(vr_ref[pl.ds(off, SEL_BLOCK), :])
            start = jnp.where(key_slot == k, j * SEL_BLOCK, start)
        k_all = jnp.concatenate(ks, axis=0).astype(BF16)
        v_all = jnp.concatenate(vs, axis=0).astype(BF16)
        spos = start + key_off
        dist = qpos - spos
        mask = (dist >= 0) & (start < n_past_blocks * SEL_BLOCK)
        s = _dot_nt(q, k_all) - slope * dist.astype(F32)
        s = jnp.where(mask, s, NEG_INF)
        m = jnp.maximum(jnp.max(s, axis=-1, keepdims=True), s_new)
        e = jnp.exp(s - m)
        e_new = jnp.exp(s_new - m)
        l = jnp.sum(e, axis=-1, keepdims=True) + e_new
        o_g = (_dot(e.astype(BF16), v_all) + e_new * vn_ref[...]) / l
        o_sel = jnp.where(grp_of_row == g, o_g, o_sel)

    n_win = wk_ref.shape[0]
    dist_w = n_win - lax.broadcasted_iota(jnp.int32, (1, n_win), 1)
    mask_w = (dist_w >= 0) & (dist_w < WINDOW)
    s_w = _dot_nt(q, wk_ref[...].astype(BF16)) - slope * dist_w.astype(F32)
    s_w = jnp.where(mask_w, s_w, NEG_INF)
    s_wn = jnp.sum(qf * wkn_ref[...], axis=1, keepdims=True)
    m_w = jnp.maximum(jnp.max(s_w, axis=-1, keepdims=True), s_wn)
    e_w = jnp.exp(s_w - m_w)
    e_wn = jnp.exp(s_wn - m_w)
    l_w = jnp.sum(e_w, axis=-1, keepdims=True) + e_wn
    o_win = (_dot(e_w.astype(BF16), wv_ref[...].astype(BF16)) + e_wn * wvn_ref[...]) / l_w

    gt = gt_ref[...]
    o = gt[:, 0:1] * oc_ref[...] + gt[:, 1:2] * o_sel + gt[:, 2:3] * o_win
    o_ref[...] = jnp.where(lane_half == grp_of_row, o, 0.0).astype(BF16)


def nsa_sample_sel(page_table, idx, pool_k, pool_v, q8, gt8, o_cmp, k_new, v_new,
                   win_k, win_v, wk_new, wv_new, *, qpos):
    nb, n_pages = page_table.shape
    n_win = win_k.shape[1]
    n_phys = pool_k.shape[0]
    blocks_per_page = PAGE_SIZE // SEL_BLOCK

    def page_spec(i):
        def index_map(b, pt, ix):
            page = jnp.clip(ix[b, i] // blocks_per_page, 0, n_pages - 1)
            return (jnp.clip(pt[b, page], 0, n_phys - 1), 0, 0)
        return pl.BlockSpec((None, SLAB, PAGE_SIZE), index_map)

    per_b = lambda r: pl.BlockSpec((None, r, SLAB), lambda b, pt, ix: (b, 0, 0))
    n_pg = N_KV * SEL_TOP
    grid_spec = pltpu.PrefetchScalarGridSpec(
        num_scalar_prefetch=2,
        grid=(nb,),
        in_specs=[page_spec(i) for i in range(n_pg)] * 2
        + [per_b(N_HEADS), per_b(N_HEADS), per_b(N_HEADS), per_b(1), per_b(1),
           per_b(n_win), per_b(n_win), per_b(1), per_b(1)],
        out_specs=per_b(N_HEADS),
        scratch_shapes=[pltpu.VMEM((n_pg * PAGE_SIZE, SLAB), F32)] * 2,
    )
    return pl.pallas_call(
        functools.partial(_nsa_sample_sel_kernel, qpos=qpos, n_past_blocks=n_pages * blocks_per_page),
        grid_spec=grid_spec,
        out_shape=jax.ShapeDtypeStruct((nb, N_HEADS, SLAB), BF16),
        compiler_params=_params(("arbitrary",)),
        name="nsa_sample_sel",
    )(page_table, idx, *([pool_k] * n_pg), *([pool_v] * n_pg), q8, gt8, o_cmp, k_new, v_new,
      win_k, win_v, wk_new, wv_new)


def _merge_ln_kernel(*refs, chunked):
    if chunked:
        (h_ref, oa_ref, uv_ref, sg_ref, wa_ref, wb_ref, wo_ref, g2_ref, b2_ref, lg_ref, lb_ref,
         sw_ref, sb_ref, o_ref) = refs
    else:
        (h_ref, oa_ref, uv_ref, sg_ref, wa_ref, wb_ref, wo_ref, g2_ref, b2_ref, lg_ref, lb_ref,
         sw_ref, sb_ref, o_ref, vn_ref) = refs
    tm = h_ref.shape[0]
    u = uv_ref[:, 0:D_B]
    vn = _layer_norm(uv_ref[:, D_B:2 * D_B], lg_ref[...], lb_ref[...])
    if chunked:
        row = lax.broadcasted_iota(jnp.int32, (CHUNK, CHUNK), 0)
        col = lax.broadcasted_iota(jnp.int32, (CHUNK, CHUNK), 1)
        vnb = vn.astype(BF16)
        chunks = []
        for c in range(tm // CHUNK):
            parts = []
            for hg in range(N_GROUPS_B):
                w = jnp.where(col <= row, sw_ref[hg], 0.0).astype(BF16)
                parts.append(_dot(w, vnb[c * CHUNK:(c + 1) * CHUNK, hg * GROUP_CH:(hg + 1) * GROUP_CH]))
            chunks.append(jnp.concatenate(parts, axis=1) + sb_ref[...])
        s = jnp.concatenate(chunks, axis=0)
    else:
        s = vn * sw_ref[...] + sb_ref[...]
        vn_ref[...] = vn
    z = u * s
    branch_a = _dot(oa_ref[...], wa_ref[...])
    branch_b = _dot(z.astype(BF16), wb_ref[...])
    merged = (sg_ref[:, 0:D_MODEL].astype(F32) * branch_a
              + sg_ref[:, D_MODEL:2 * D_MODEL].astype(F32) * branch_b)
    y = ALPHA * h_ref[...] + _dot(merged.astype(BF16), wo_ref[...])
    o_ref[...] = _layer_norm(y, g2_ref[...], b2_ref[...])


def merge_ln(h, oa, uv, sgab, wa, wb, wo, g2, b2, lg, lb, sw, sb, *, tm, chunked):
    m, d = h.shape
    row = lambda n: pl.BlockSpec((tm, n), lambda i: (i, 0))
    full = lambda a: pl.BlockSpec(a.shape, lambda i: (0,) * a.ndim)
    vec = lambda a: a.reshape(1, -1)
    g2, b2, lg, lb = vec(g2), vec(b2), vec(lg), vec(lb)
    out_shape = [jax.ShapeDtypeStruct((m, d), F32)]
    out_specs = [row(d)]
    if not chunked:
        out_shape.append(jax.ShapeDtypeStruct((m, D_B), F32))
        out_specs.append(row(D_B))
    res = pl.pallas_call(
        functools.partial(_merge_ln_kernel, chunked=chunked),
        grid=(m // tm,),
        in_specs=[row(d), row(Q_EXP), row(2 * D_B), row(2 * D_MODEL), full(wa), full(wb), full(wo),
                  full(g2), full(b2), full(lg), full(lb), full(sw), full(sb)],
        out_specs=out_specs,
        out_shape=out_shape,
        compiler_params=_params(("parallel",)),
        name="merge_ln",
    )(h, oa, uv, sgab, wa, wb, wo, g2, b2, lg, lb, sw, sb)
    return res if not chunked else res[0]


def _group_mask():
    return (np.arange(N_HEADS)[:, None] // GROUP == np.arange(N_KV)[None, :]).astype(np.float32)


def _split_w_in(w_in):
    sizes = (Q_W,) + (KV_W,) * 6 + (GATE_W, D_B, D_B, D_MODEL, D_MODEL)
    cuts = np.cumsum(sizes)[:-1].tolist()
    q, kc, vc, ks, vs, kw, vw, g, u, v, ga, gb = jnp.split(w_in, cuts, axis=1)
    d = w_in.shape[0]
    gm = jnp.asarray(_group_mask())
    wq = (q.reshape(d, N_HEADS, 1, HEAD_DIM) * gm[None, :, :, None]).reshape(d, Q_EXP)
    wkv = jnp.concatenate([kc, vc, ks, vs, kw, vw], axis=1)
    wgt = jnp.pad(g, ((0, 0), (0, 128 - GATE_W)))
    wuv = jnp.concatenate([u, v], axis=1)
    wgab = jnp.concatenate([ga, gb], axis=1)
    return tuple(w.astype(BF16) for w in (wq, wq * LOG2E, wkv, wgt, wuv, wgab))


def _expand_branch_a(w_a):
    gm = jnp.asarray(_group_mask())
    w = w_a.reshape(N_HEADS, 1, HEAD_DIM, w_a.shape[1]) * gm[:, :, None, None]
    return w.reshape(Q_EXP, w_a.shape[1]).astype(BF16)


def _cmp_weights(pe, w1, w2):
    n_sub = CMP_LEN // CMP_STRIDE
    w1r = w1.reshape(n_sub, CMP_STRIDE, HEAD_DIM, CMP_HIDDEN)
    eye = jnp.eye(N_KV, dtype=w1.dtype)
    w_ab = jnp.einsum('spdc,gh->pgdshc', w1r, eye).reshape(SUB_W, n_sub * N_KV * CMP_HIDDEN)
    w2x = jnp.einsum('cd,gh->gchd', w2, eye).reshape(N_KV * CMP_HIDDEN, SLAB)
    pe8 = jnp.pad(pe.reshape(1, -1), ((0, 7), (0, 0)))
    return w_ab.astype(BF16), pe8.astype(BF16), w1.astype(BF16), w2x.astype(BF16)


def _importance_matrix(n_cmp, n_cand):
    i = np.arange(n_cmp)[:, None]
    j = np.arange(n_cand)[None, :]
    a = (i >= CMP_RATIO * j - 1) & (i <= CMP_RATIO * j + CMP_RATIO - 1)
    return jnp.asarray(a.astype(np.float32)).astype(BF16)


def _compress_rows(rows, cw):
    w_ab, pe8, w1b, w2x = cw
    nb = rows.shape[0]
    ident = jnp.arange(nb, dtype=jnp.int32).reshape(nb, 1)
    return cmp_fin(cmp_ab(rows, ident, w_ab, pages_per_step=1), None, pe8, w1b, w2x)


def _compress_paged(pool, page_table, tail, cw, *, pages_per_step):
    w_ab, pe8, w1b, w2x = cw
    ab = cmp_ab_paged(pool, page_table, w_ab, pages_per_step=pages_per_step)
    one = jnp.zeros((1, 1), jnp.int32)
    ab_tail = cmp_ab(tail.reshape(1, -1, SUB_W), one, w_ab, pages_per_step=1)
    ab_tail = ab_tail.reshape(tail.shape[0], tail.shape[1], -1)
    return cmp_fin(ab, ab_tail, pe8, w1b, w2x)


def _position_minor(a):
    n, p = a.shape[0], a.shape[1]
    return jnp.transpose(a, (0, 2, 3, 1)).reshape(n, SLAB, p)


def _position_major(a_t):
    n, _, p = a_t.shape
    return jnp.transpose(a_t.reshape(n, N_KV, HEAD_DIM, p), (0, 3, 1, 2))


def kernel(x_prompt, x_sample, cache_cmp_k, cache_cmp_v, cache_sel_k, cache_sel_v, cache_win_k,
           cache_win_v, page_table, ffn1_w_in, ffn1_w_out, ln1_g, ln1_b, w_in, cmp_pe_k, cmp_w1_k,
           cmp_w2_k, cmp_pe_v, cmp_w1_v, cmp_w2_v, gmlp_ln_g, gmlp_ln_b, spatial_w, spatial_b,
           w_branch_a, w_branch_b, w_out, ln2_g, ln2_b, ffn2_w_in, ffn2_w_out, ln3_g, ln3_b):
    nb, t, d = x_prompt.shape
    db, n_new, _ = x_sample.shape
    assert n_new == 1 and t % SEL_TILE == 0
    n_pages = page_table.shape[1]
    past = n_pages * PAGE_SIZE
    n_phys = cache_cmp_k.shape[0]
    m = nb * t

    f1_in, f1_out = ffn1_w_in.astype(BF16), ffn1_w_out.astype(BF16)
    f2_in, f2_out = ffn2_w_in.astype(BF16), ffn2_w_out.astype(BF16)
    wq, wq_base2, *proj_rest = _split_w_in(w_in)
    wa = _expand_branch_a(w_branch_a)
    wb = w_branch_b.astype(BF16)
    wo = w_out.astype(BF16)
    cw_k = _cmp_weights(cmp_pe_k, cmp_w1_k, cmp_w2_k)
    cw_v = _cmp_weights(cmp_pe_v, cmp_w1_v, cmp_w2_v)

    h = ffn_ln(x_prompt.reshape(m, d), f1_in, f1_out, ln1_g, ln1_b, tm=1024)
    (qt, kct, vct, kst, vst, kwt, vwt, kcb, vcb, ksb, kwb, vstb, vwtb, gtt, uv,
     sgab) = mixer_proj_t(h, wq_base2, *proj_rest, nb=nb, tm=512)
    n_cmp = t // CMP_STRIDE
    kcc = _compress_rows(kcb.reshape(nb, n_cmp, SUB_W), cw_k)
    vcc = _compress_rows(vcb.reshape(nb, n_cmp, SUB_W), cw_v)
    imp_t = _importance_matrix(n_cmp, t // SEL_BLOCK).T
    o_a = nsa_prompt(qt, gtt, kcc, jnp.swapaxes(vcc, 1, 2), ksb.reshape(nb, t, SLAB), vstb,
                     kwb.reshape(nb, t, SLAB), vwtb, imp_t)
    sb_tile = jnp.repeat(spatial_b.T, GROUP_CH, axis=1)
    h2 = merge_ln(h, o_a.reshape(m, Q_EXP), uv, sgab, wa, wb, wo, ln2_g, ln2_b, gmlp_ln_g,
                  gmlp_ln_b, spatial_w, sb_tile, tm=512, chunked=True)
    y_prompt = ffn_ln(h2, f2_in, f2_out, ln3_g, ln3_b, tm=1024).reshape(nb, t, d)
    wkeep = min(WINDOW, t)
    p_outs = tuple(_position_major(a) for a in
                   (kct, vct, kst, vst, kwt[:, :, t - wkeep:], vwt[:, :, t - wkeep:]))

    hs = ffn_ln(x_sample.reshape(db, d), f1_in, f1_out, ln1_g, ln1_b, tm=db)
    (qs, kcs, vcs, kss, vss, kws, vws, _, gts, uvs, sgabs) = mixer_proj(hs, wq, *proj_rest, tm=db)
    t_pad = -(-(past + n_new) // SEL_BLOCK) * SEL_BLOCK
    n_tail = (t_pad - past) // CMP_STRIDE
    tail_rows = 16

    def tail_of(new):
        flat = jnp.pad(new, ((0, 0), (0, tail_rows * SUB_W - SLAB)))
        return flat.reshape(db, tail_rows, SUB_W)

    assert n_tail <= tail_rows
    pps = min(64, n_pages)
    kcc_s = _compress_paged(_position_minor(cache_cmp_k), page_table, tail_of(kcs), cw_k,
                            pages_per_step=pps)
    vcc_s = _compress_paged(_position_minor(cache_cmp_v), page_table, tail_of(vcs), cw_v,
                            pages_per_step=pps)
    n_sel_s = t_pad // SEL_BLOCK
    n_cand = -(-n_sel_s // 128) * 128
    imp_s = _importance_matrix(kcc_s.shape[1], n_cand)
    q8 = qs.reshape(db, N_HEADS, SLAB)
    o_cmp_s, idx_s = nsa_sample_cmp(q8, kcc_s, vcc_s, imp_s, qpos=past, n_sel=n_sel_s)
    idx_flat = idx_s[:, :N_KV, :SEL_TOP].reshape(db, N_KV * SEL_TOP)
    gt8 = jnp.pad(gts[:, :GATE_W].reshape(db, N_HEADS, 3), ((0, 0), (0, 0), (0, 125)))
    row3 = lambda a: a.reshape(db, 1, SLAB)
    win3 = lambda a: a.reshape(db, -1, SLAB)
    o_as = nsa_sample_sel(page_table, idx_flat, _position_minor(cache_sel_k),
                          _position_minor(cache_sel_v),
                          q8, gt8, o_cmp_s, row3(kss), row3(vss), win3(cache_win_k),
                          win3(cache_win_v), row3(kws), row3(vws), qpos=past)
    sw_row = jnp.repeat(spatial_w[:, 0, 0], GROUP_CH).reshape(1, D_B)
    sb_row = jnp.repeat(spatial_b[:, 0], GROUP_CH).reshape(1, D_B)
    hs2, vn_s = merge_ln(hs, o_as.reshape(db, Q_EXP), uvs, sgabs, wa, wb, wo, ln2_g, ln2_b,
                         gmlp_ln_g, gmlp_ln_b, sw_row, sb_row, tm=db, chunked=False)
    y_sample = ffn_ln(hs2, f2_in, f2_out, ln3_g, ln3_b, tm=db).reshape(db, n_new, d)
    kv4s = lambda a: a.reshape(db, n_new, N_KV, HEAD_DIM)
    wbuf = cache_win_k.shape[1]
    s_win_k = jnp.concatenate([cache_win_k, kv4s(kws)], axis=1)[:, -wbuf:]
    s_win_v = jnp.concatenate([cache_win_v, kv4s(vws)], axis=1)[:, -wbuf:]

    return (y_prompt, y_sample) + p_outs + (kv4s(kcs), kv4s(vcs), kv4s(kss), kv4s(vss),
                                            s_win_k, s_win_v, vn_s.reshape(db, n_new, D_B))
```

```python
import functools

import numpy as np
import jax
import jax.numpy as jnp
from jax import lax
from jax.experimental import pallas as pl
from jax.experimental.pallas import tpu as pltpu

F32 = jnp.float32
BF16 = jnp.bfloat16

D_MODEL = 1024
N_HEADS = 8
HEAD_DIM = 64
N_KV = 2
GROUP = N_HEADS // N_KV
PAGE_SIZE = 128
CMP_LEN = 32
CMP_STRIDE = 16
CMP_HIDDEN = 128
SEL_BLOCK = 64
CMP_RATIO = SEL_BLOCK // CMP_STRIDE
SEL_TOP = 16
WINDOW = 512
Q_BLOCK = 128
FORCE_SCORE = 1e4
D_B = 512
N_GROUPS_B = 4
GROUP_CH = D_B // N_GROUPS_B
CHUNK = 128
D_FF = 2816
DEPTH = 1
ALPHA = (2.0 * DEPTH) ** 0.25
LN_EPS = 1e-5
NEG_INF = -1e30
REMOVED = -3e38
LOG2E = float(np.log2(np.e))
Q_W = N_HEADS * HEAD_DIM
KV_W = N_KV * HEAD_DIM
GATE_W = N_HEADS * 3
SLAB = N_KV * HEAD_DIM
Q_EXP = N_HEADS * SLAB
SUB_W = CMP_STRIDE * SLAB
VMEM_LIMIT = 40 * 1024 * 1024
SEL_TILE = 1024
CMP_PAGE_GROUPS = 4


def _params(sem, vmem=VMEM_LIMIT, flags=None):
    return pltpu.CompilerParams(dimension_semantics=sem, vmem_limit_bytes=vmem, flags=flags)


def _layer_norm(x, g, b):
    mu = jnp.mean(x, axis=-1, keepdims=True)
    xc = x - mu
    var = jnp.mean(xc * xc, axis=-1, keepdims=True)
    return xc * lax.rsqrt(var + LN_EPS) * g + b


def _dot(a, b):
    return jnp.dot(a, b, preferred_element_type=F32)


def _dot_nt(a, b):
    return lax.dot_general(a, b, (((1,), (1,)), ((), ())), preferred_element_type=F32)


def _split3(x):
    hi = x.astype(BF16)
    r1 = x - hi.astype(F32)
    mid = r1.astype(BF16)
    lo = (r1 - mid.astype(F32)).astype(BF16)
    return hi, mid, lo


def _dot_exact_rhs(x, a_bf16):
    hi, mid, lo = _split3(x)
    return _dot(hi, a_bf16) + _dot(mid, a_bf16) + _dot(lo, a_bf16)


def _ffn_ln_kernel(x_ref, wg_ref, wu_ref, wo_ref, g_ref, b_ref, o_ref, xb_ref, acc_ref, *, n_f):
    f = pl.program_id(1)

    @pl.when(f == 0)
    def _():
        xb_ref[...] = x_ref[...].astype(BF16)
        acc_ref[...] = jnp.zeros_like(acc_ref)

    xb = xb_ref[...]
    gate = _dot(xb, wg_ref[...])
    up = _dot(xb, wu_ref[...])
    hid = (gate * jax.nn.sigmoid(gate)) * up
    acc_ref[...] += _dot(hid.astype(BF16), wo_ref[...])

    @pl.when(f == n_f - 1)
    def _():
        y = ALPHA * x_ref[...] + 0.5 * acc_ref[...]
        o_ref[...] = _layer_norm(y, g_ref[...], b_ref[...])


def ffn_ln(x, w_in_b, w_out_b, g, b, *, tm, tf=256):
    m, d = x.shape
    d_ff = w_out_b.shape[0]
    n_f = d_ff // tf
    return pl.pallas_call(
        functools.partial(_ffn_ln_kernel, n_f=n_f),
        grid=(m // tm, n_f),
        in_specs=[
            pl.BlockSpec((tm, d), lambda i, f: (i, 0)),
            pl.BlockSpec((d, tf), lambda i, f: (0, f)),
            pl.BlockSpec((d, tf), lambda i, f: (0, f + n_f)),
            pl.BlockSpec((tf, d), lambda i, f: (f, 0)),
            pl.BlockSpec((1, d), lambda i, f: (0, 0)),
            pl.BlockSpec((1, d), lambda i, f: (0, 0)),
        ],
        out_specs=pl.BlockSpec((tm, d), lambda i, f: (i, 0)),
        out_shape=jax.ShapeDtypeStruct((m, d), F32),
        scratch_shapes=[pltpu.VMEM((tm, d), BF16), pltpu.VMEM((tm, d), F32)],
        compiler_params=_params(("parallel", "arbitrary")),
        name="ffn_ln",
    )(x, w_in_b, w_in_b, w_out_b, g.reshape(1, d), b.reshape(1, d))


def _mixer_proj_kernel(h_ref, wq_ref, wkv_ref, wgt_ref, wuv_ref, wgab_ref,
                       q_ref, kc_ref, vc_ref, ks_ref, vs_ref, kw_ref, vw_ref,
                       kvb_ref, gt_ref, uv_ref, sgab_ref):
    hb = h_ref[...].astype(BF16)
    q_ref[...] = (_dot(hb, wq_ref[...]) * (HEAD_DIM ** -0.5)).astype(BF16)
    kv = _dot(hb, wkv_ref[...])
    for j, ref in enumerate((kc_ref, vc_ref, ks_ref, vs_ref, kw_ref, vw_ref)):
        ref[...] = kv[:, j * SLAB:(j + 1) * SLAB]
    kvb_ref[...] = kv.astype(BF16)
    gt_ref[...] = jax.nn.sigmoid(_dot(hb, wgt_ref[...]))
    uv_ref[...] = jax.nn.gelu(_dot(hb, wuv_ref[...]))
    sgab_ref[...] = jax.nn.sigmoid(_dot(hb, wgab_ref[...])).astype(BF16)


def mixer_proj(h, wq, wkv, wgt, wuv, wgab, *, tm):
    m, d = h.shape
    row = lambda n: pl.BlockSpec((tm, n), lambda i: (i, 0))
    full = lambda w: pl.BlockSpec(w.shape, lambda i: (0, 0))
    out_shape = (
        [jax.ShapeDtypeStruct((m, Q_EXP), BF16)]
        + [jax.ShapeDtypeStruct((m, SLAB), F32)] * 6
        + [jax.ShapeDtypeStruct((m, 6 * SLAB), BF16),
           jax.ShapeDtypeStruct((m, 128), F32),
           jax.ShapeDtypeStruct((m, 2 * D_B), F32),
           jax.ShapeDtypeStruct((m, 2 * D_MODEL), BF16)])
    out_specs = ([row(Q_EXP)] + [row(SLAB)] * 6
                 + [row(6 * SLAB), row(128), row(2 * D_B), row(2 * D_MODEL)])
    return pl.pallas_call(
        _mixer_proj_kernel,
        grid=(m // tm,),
        in_specs=[row(d), full(wq), full(wkv), full(wgt), full(wuv), full(wgab)],
        out_specs=out_specs,
        out_shape=out_shape,
        compiler_params=_params(("parallel",)),
        name="mixer_proj",
    )(h, wq, wkv, wgt, wuv, wgab)


def _mixer_proj_t_kernel(h_ref, wq_ref, wkv_ref, wgt_ref, wuv_ref, wgab_ref,
                         qt_ref, kct_ref, vct_ref, kst_ref, vst_ref, kwt_ref, vwt_ref,
                         kcb_ref, vcb_ref, ksb_ref, kwb_ref, vstb_ref, vwtb_ref,
                         gtt_ref, uv_ref, sgab_ref):
    hb = h_ref[...].astype(BF16)
    q = _dot(hb, wq_ref[...]) * (HEAD_DIM ** -0.5)
    zeros = jnp.zeros((HEAD_DIM, h_ref.shape[0]), BF16)
    for pair in range(N_HEADS // 2):
        pair_t = q[:, pair * SLAB:(pair + 1) * SLAB].T.astype(BF16)
        for e in range(2):
            h = 2 * pair + e
            g = h // GROUP
            qt_ref[h * SLAB + g * HEAD_DIM:h * SLAB + (g + 1) * HEAD_DIM, :] = (
                pair_t[e * HEAD_DIM:(e + 1) * HEAD_DIM, :])
            qt_ref[h * SLAB + (1 - g) * HEAD_DIM:h * SLAB + (2 - g) * HEAD_DIM, :] = zeros
    kv = _dot(hb, wkv_ref[...])
    slabs = [kv[:, j * SLAB:(j + 1) * SLAB] for j in range(6)]
    for slab, ref in zip(slabs, (kct_ref, vct_ref, kst_ref, vst_ref, kwt_ref, vwt_ref)):
        ref[...] = slab.T
    kcb_ref[...] = slabs[0].astype(BF16)
    vcb_ref[...] = slabs[1].astype(BF16)
    ksb_ref[...] = slabs[2].astype(BF16)
    kwb_ref[...] = slabs[4].astype(BF16)
    vstb_ref[...] = slabs[3].T.astype(BF16)
    vwtb_ref[...] = slabs[5].T.astype(BF16)
    gtt_ref[...] = jax.nn.sigmoid(_dot(hb, wgt_ref[...])).T
    uv_ref[...] = jax.nn.gelu(_dot(hb, wuv_ref[...]))
    sgab_ref[...] = jax.nn.sigmoid(_dot(hb, wgab_ref[...])).astype(BF16)


def mixer_proj_t(h, wq, wkv, wgt, wuv, wgab, *, nb, tm):
    m, d = h.shape
    t = m // nb
    per_b = t // tm
    row = lambda n: pl.BlockSpec((tm, n), lambda i: (i, 0))
    col = lambda n: pl.BlockSpec((None, n, tm), lambda i: (i // per_b, 0, i % per_b))
    full = lambda w: pl.BlockSpec(w.shape, lambda i: (0, 0))
    sds = jax.ShapeDtypeStruct
    out_shape = ([sds((nb, Q_EXP, t), BF16)] + [sds((nb, SLAB, t), F32)] * 6
                 + [sds((m, SLAB), BF16)] * 4 + [sds((nb, SLAB, t), BF16)] * 2
                 + [sds((nb, 128, t), F32), sds((m, 2 * D_B), F32), sds((m, 2 * D_MODEL), BF16)])
    out_specs = ([col(Q_EXP)] + [col(SLAB)] * 6 + [row(SLAB)] * 4 + [col(SLAB)] * 2
                 + [col(128), row(2 * D_B), row(2 * D_MODEL)])
    return pl.pallas_call(
        _mixer_proj_t_kernel,
        grid=(m // tm,),
        in_specs=[row(d), full(wq), full(wkv), full(wgt), full(wuv), full(wgab)],
        out_specs=out_specs,
        out_shape=out_shape,
        compiler_params=_params(("parallel",)),
        name="mixer_proj_t",
    )(h, wq, wkv, wgt, wuv, wgab)


def _cmp_ab_kernel(pt_ref, *refs, n_pages):
    del pt_ref
    page_refs = refs[:n_pages]
    w_ref = refs[n_pages]
    o_ref = refs[n_pages + 1]
    if n_pages == 1:
        x = page_refs[0][...]
    else:
        x = jnp.concatenate([r[...] for r in page_refs], axis=0)
    o_ref[...] = _dot(x.astype(BF16), w_ref[...])


def cmp_ab(pool, page_table, w_ab, *, pages_per_step):
    nb, n_pages = page_table.shape
    rows = pool.shape[1]
    pp = pages_per_step
    n_steps = n_pages // pp

    def page_spec(k):
        return pl.BlockSpec((None, rows, SUB_W), lambda b, s, pt: (pt[b, s * pp + k], 0, 0))

    grid_spec = pltpu.PrefetchScalarGridSpec(
        num_scalar_prefetch=1,
        grid=(nb, n_steps),
        in_specs=[page_spec(k) for k in range(pp)]
        + [pl.BlockSpec(w_ab.shape, lambda b, s, pt: (0, 0))],
        out_specs=pl.BlockSpec((None, pp * rows, w_ab.shape[1]), lambda b, s, pt: (b, s, 0)),
    )
    return pl.pallas_call(
        functools.partial(_cmp_ab_kernel, n_pages=pp),
        grid_spec=grid_spec,
        out_shape=jax.ShapeDtypeStruct((nb, n_pages * rows, w_ab.shape[1]), F32),
        compiler_params=_params(("parallel", "arbitrary")),
        name="cmp_ab",
    )(page_table, *([pool] * pp), w_ab)


def _cmp_ab_paged_kernel(pt_ref, *refs, n_pages):
    del pt_ref
    page_refs = refs[:n_pages]
    w_ref, o_ref, rows_ref = refs[n_pages:]
    per_group = n_pages // CMP_PAGE_GROUPS
    n_sub = per_group * PAGE_SIZE // CMP_STRIDE
    for grp in range(CMP_PAGE_GROUPS):
        base = grp * per_group * PAGE_SIZE
        for k in range(per_group):
            rows_ref[base + k * PAGE_SIZE:base + (k + 1) * PAGE_SIZE, :] = (
                page_refs[grp * per_group + k][...].T)
        x = jnp.concatenate([rows_ref[pl.ds(base + p, n_sub, stride=CMP_STRIDE), :].astype(BF16)
                             for p in range(CMP_STRIDE)], axis=1)
        o_ref[grp * n_sub:(grp + 1) * n_sub, :] = _dot(x, w_ref[...])


def cmp_ab_paged(pool, page_table, w_ab, *, pages_per_step):
    nb, n_pages = page_table.shape
    pp = pages_per_step
    sub_per_page = PAGE_SIZE // CMP_STRIDE

    n_phys = pool.shape[0]

    def page_spec(k):
        return pl.BlockSpec((None, SLAB, PAGE_SIZE),
                            lambda b, s, pt: (jnp.clip(pt[b, s * pp + k], 0, n_phys - 1), 0, 0))

    grid_spec = pltpu.PrefetchScalarGridSpec(
        num_scalar_prefetch=1,
        grid=(nb, n_pages // pp),
        in_specs=[page_spec(k) for k in range(pp)]
        + [pl.BlockSpec(w_ab.shape, lambda b, s, pt: (0, 0))],
        out_specs=pl.BlockSpec((None, pp * sub_per_page, w_ab.shape[1]), lambda b, s, pt: (b, s, 0)),
        scratch_shapes=[pltpu.VMEM((pp * PAGE_SIZE, SLAB), F32)],
    )
    return pl.pallas_call(
        functools.partial(_cmp_ab_paged_kernel, n_pages=pp),
        grid_spec=grid_spec,
        out_shape=jax.ShapeDtypeStruct((nb, n_pages * sub_per_page, w_ab.shape[1]), F32),
        compiler_params=_params(("parallel", "arbitrary")),
        name="cmp_ab_paged",
    )(page_table, *([pool] * pp), w_ab)


def _cmp_fin_kernel(*refs, has_tail):
    if has_tail:
        ab_ref, tail_ref, pe_ref, w1_ref, w2_ref, o_ref = refs
    else:
        ab_ref, pe_ref, w1_ref, w2_ref, o_ref = refs
    hw = 2 * CMP_HIDDEN
    n = ab_ref.shape[0]
    pe_h = _dot(pe_ref[...], w1_ref[...])[0:1, :]
    bias = jnp.concatenate([pe_h, pe_h], axis=1)
    w2 = w2_ref[...]
    first = ab_ref[:, 0:hw]
    second = pltpu.roll(ab_ref[:, hw:2 * hw], n - 1, 0)
    if has_tail:
        t_first = tail_ref[:, 0:hw]
        t_second = tail_ref[:, hw:2 * hw]
        nt = tail_ref.shape[0]
        is_last = lax.broadcasted_iota(jnp.int32, (n, 1), 0) == n - 1
        second = jnp.where(is_last, t_second[0:1, :], second)
        t_hid = t_first + pltpu.roll(t_second, nt - 1, 0) + bias
        o_ref[n:n + nt, :] = _dot(jax.nn.gelu(t_hid).astype(BF16), w2).astype(BF16)
        n_out = o_ref.shape[0]
        if n_out > n + nt:
            o_ref[n + nt:n_out, :] = jnp.zeros((n_out - n - nt, SLAB), BF16)
    hid = first + second + bias
    o_ref[0:n, :] = _dot(jax.nn.gelu(hid).astype(BF16), w2).astype(BF16)


def cmp_fin(ab, tail, pe8, w1b, w2x):
    nb, n, w = ab.shape
    has_tail = tail is not None
    nt = tail.shape[1] if has_tail else 0
    full = lambda a: pl.BlockSpec(a.shape, lambda b: (0, 0))
    in_specs = [pl.BlockSpec((None, n, w), lambda b: (b, 0, 0))]
    args = [ab]
    if has_tail:
        in_specs.append(pl.BlockSpec((None, nt, w), lambda b: (b, 0, 0)))
        args.append(tail)
    in_specs += [full(pe8), full(w1b), full(w2x)]
    args += [pe8, w1b, w2x]
    n_out = -(-(n + nt) // 128) * 128 if has_tail else n
    return pl.pallas_call(
        functools.partial(_cmp_fin_kernel, has_tail=has_tail),
        grid=(nb,),
        in_specs=in_specs,
        out_specs=pl.BlockSpec((None, n_out, SLAB), lambda b: (b, 0, 0)),
        out_shape=jax.ShapeDtypeStruct((nb, n_out, SLAB), BF16),
        compiler_params=_params(("parallel",)),
        name="cmp_fin",
    )(*args)


def _slope_of_head(h):
    return 2.0 ** (-(h + 1))


def _slope_column():
    head = lax.broadcasted_iota(jnp.int32, (N_HEADS, 1), 0)
    slope = jnp.zeros((N_HEADS, 1), F32)
    for h in range(N_HEADS):
        slope = jnp.where(head == h, _slope_of_head(h), slope)
    return slope


def _top_k_mask_t(score_t, k):
    n = score_t.shape[0]
    cand_idx = lax.broadcasted_iota(jnp.int32, score_t.shape, 0).astype(F32)

    def body(_, carry):
        s, sel = carry
        m = jnp.max(s, axis=0, keepdims=True)
        first = jnp.min(jnp.where(s == m, cand_idx, float(n)), axis=0, keepdims=True)
        pick = cand_idx == first
        return jnp.where(pick, REMOVED, s), jnp.where(pick, 1.0, sel)

    _, sel = lax.fori_loop(0, k, body, (score_t, jnp.zeros_like(score_t)), unroll=True)
    return sel


def _softmax_masked(s, mask):
    s = jnp.where(mask, s, NEG_INF)
    m = jnp.max(s, axis=-1, keepdims=True)
    e = jnp.exp(s - m)
    return jnp.where(mask, e / jnp.sum(e, axis=-1, keepdims=True), 0.0)


def _softmax_masked_t(s, mask):
    s = jnp.where(mask, s, NEG_INF)
    m = jnp.max(s, axis=0, keepdims=True)
    e = jnp.exp2(s - m)
    return jnp.where(mask, e * (1.0 / jnp.sum(e, axis=0, keepdims=True)), 0.0)


def _nsa_prompt_kernel(qt_ref, gtt_ref, kc_ref, vct_ref, ks_ref, vst_ref, kw_ref, vwt_ref, impt_ref,
                       o_ref, selt_ref, bias_c_ref, bias_w_ref, bias_s_ref):
    qb = pl.program_id(1)
    s0 = qb * Q_BLOCK
    nq = Q_BLOCK
    hq = N_HEADS * nq
    nc = kc_ref.shape[0]
    n_sel = ks_ref.shape[0] // SEL_BLOCK
    tk = SEL_TILE
    assert HEAD_DIM == 64 and SEL_BLOCK == 64 and Q_BLOCK == 128

    lane = lax.broadcasted_iota(jnp.int32, (1, hq), 1)
    q_loc = lane & (nq - 1)
    qpos_f = (s0 + q_loc).astype(F32)
    qpos_q = s0 + lax.broadcasted_iota(jnp.int32, (1, nq), 1)
    slope = jnp.zeros((1, hq), F32)
    for h in range(N_HEADS):
        slope = jnp.where((lane >> 7) == h, _slope_of_head(h) * LOG2E, slope)
    n_win = WINDOW + nq
    k_loc = lax.broadcasted_iota(jnp.int32, (tk, 1), 0)

    @pl.when(qb == 0)
    def _():
        q_loc_f = q_loc.astype(F32)
        c_end = lax.broadcasted_iota(jnp.int32, (nc, 1), 0) * CMP_STRIDE + (CMP_LEN - 1)
        bias_c_ref[...] = slope * (q_loc_f - c_end.astype(F32))
        w_loc = lax.broadcasted_iota(jnp.int32, (n_win, 1), 0)
        bias_w_ref[...] = slope * (q_loc_f - w_loc.astype(F32))
        bias_s_ref[...] = slope * k_loc.astype(F32)

    j_idx = lax.broadcasted_iota(jnp.int32, (n_sel, 1), 0)
    cur = qpos_q >> 6
    forced = (j_idx == 0) | (j_idx == cur) | (j_idx == cur - 1)
    valid = (j_idx * SEL_BLOCK) <= qpos_q

    w0 = pl.multiple_of(jnp.maximum(s0 - WINDOW, 0), Q_BLOCK)
    kw = kw_ref[pl.ds(w0, n_win), :]
    vwt = vwt_ref[:, pl.ds(w0, n_win)]

    assert tk % nq == 0
    n_full = s0 // tk
    blocks_per_tile = tk // SEL_BLOCK

    qt = jnp.concatenate([qt_ref[h * SLAB:(h + 1) * SLAB, :] for h in range(N_HEADS)],
                         axis=1)

    bias_c = bias_c_ref[...]
    mask_c = bias_c >= slope * (-s0).astype(F32)
    p = _softmax_masked_t(_dot(kc_ref[...], qt) - bias_c, mask_c)
    o_cmp = _dot(vct_ref[...], p.astype(BF16))

    imp_t = impt_ref[...]
    scores = []
    for g in range(N_KV):
        p_sum = p[:, g * GROUP * nq:(g * GROUP + 1) * nq]
        for r in range(1, GROUP):
            p_sum = p_sum + p[:, (g * GROUP + r) * nq:(g * GROUP + r + 1) * nq]
        hi, mid, lo = _split3(p_sum)
        p_slc = _dot(imp_t, hi) + _dot(imp_t, mid) + _dot(imp_t, lo)
        scores.append(jnp.where(valid, jnp.where(forced, FORCE_SCORE, p_slc), NEG_INF))
    sel = _top_k_mask_t(jnp.concatenate(scores, axis=1), min(SEL_TOP, n_sel))
    for g in range(N_KV):
        selt_ref[g] = sel[:, g * nq:(g + 1) * nq]


    def sel_tile(t, carry, diagonal):
        m_old, l_old, acc_old = carry
        kt = pl.multiple_of(t * tk, tk)
        k = ks_ref[pl.ds(kt, tk), :]
        vt = vst_ref[:, pl.ds(kt, tk)]
        s_rel = _dot(k, qt) + bias_s_ref[...]
        shift = slope * (kt.astype(F32) - qpos_f)
        cols = []
        for g in range(N_KV):
            picked = jnp.concatenate(
                [jnp.broadcast_to(selt_ref[g, pl.ds(t * blocks_per_tile + jj, 1), :],
                                  (SEL_BLOCK, nq))
                 for jj in range(blocks_per_tile)], axis=0) > 0.5
            if diagonal:
                picked = picked & ((kt + k_loc) <= qpos_q)
            for r in range(GROUP):
                h = g * GROUP + r
                cols.append(jnp.where(picked, s_rel[:, h * nq:(h + 1) * nq], NEG_INF))
        s_rel = jnp.concatenate(cols, axis=1)
        m_new = jnp.maximum(m_old, jnp.max(s_rel, axis=0, keepdims=True) + shift)
        alpha = jnp.exp2(m_old - m_new)
        p_t = jnp.exp2(s_rel - (m_new - shift))
        l_new = alpha * l_old + jnp.sum(p_t, axis=0, keepdims=True)
        acc_new = alpha * acc_old + _dot(vt, p_t.astype(BF16))
        return m_new, l_new, acc_new

    init = (jnp.full((1, hq), NEG_INF, F32), jnp.zeros((1, hq), F32), jnp.zeros((SLAB, hq), F32))
    state = lax.fori_loop(0, n_full, functools.partial(sel_tile, diagonal=False), init)
    _, l_sel, acc_sel = sel_tile(n_full, state, diagonal=True)
    o_sel = acc_sel * (1.0 / l_sel)

    bias_w = bias_w_ref[...]
    lead = (s0 - w0).astype(F32)
    mask_w = (bias_w >= slope * (-lead)) & (bias_w < slope * (WINDOW - lead))
    o_win = _dot(vwt, _softmax_masked_t(_dot(kw, qt) - bias_w, mask_w).astype(BF16))

    for pair in range(N_HEADS // 2):
        halves = []
        for h in (2 * pair, 2 * pair + 1):
            cols = slice(h * nq, (h + 1) * nq)
            rows = slice((h // GROUP) * HEAD_DIM, (h // GROUP + 1) * HEAD_DIM)
            halves.append(gtt_ref[3 * h:3 * h + 1, :] * o_cmp[rows, cols]
                          + gtt_ref[3 * h + 1:3 * h + 2, :] * o_sel[rows, cols]
                          + gtt_ref[3 * h + 2:3 * h + 3, :] * o_win[rows, cols])
        o_ref[:, pair * SLAB:(pair + 1) * SLAB] = jnp.concatenate(halves, axis=0).T.astype(BF16)


def nsa_prompt(qt, gtt, kc, vct, ks, vst, kw, vwt, imp_t):
    nb, _, t = qt.shape
    nc = kc.shape[1]
    rows = pl.BlockSpec((None, t, SLAB), lambda b, i: (b, 0, 0))
    cols = pl.BlockSpec((None, SLAB, t), lambda b, i: (b, 0, 0))
    return pl.pallas_call(
        _nsa_prompt_kernel,
        grid=(nb, t // Q_BLOCK),
        in_specs=[
            pl.BlockSpec((None, Q_EXP, Q_BLOCK), lambda b, i: (b, 0, i)),
            pl.BlockSpec((None, 128, Q_BLOCK), lambda b, i: (b, 0, i)),
            pl.BlockSpec((None, nc, SLAB), lambda b, i: (b, 0, 0)),
            pl.BlockSpec((None, SLAB, nc), lambda b, i: (b, 0, 0)),
            rows, cols, rows, cols,
            pl.BlockSpec(imp_t.shape, lambda b, i: (0, 0)),
        ],
        out_specs=pl.BlockSpec((None, Q_BLOCK, Q_W), lambda b, i: (b, i, 0)),
        out_shape=jax.ShapeDtypeStruct((nb, t, Q_W), BF16),
        scratch_shapes=[pltpu.VMEM((N_KV, t // SEL_BLOCK, Q_BLOCK), F32),
                        pltpu.VMEM((nc, N_HEADS * Q_BLOCK), F32),
                        pltpu.VMEM((WINDOW + Q_BLOCK, N_HEADS * Q_BLOCK), F32),
                        pltpu.VMEM((SEL_TILE, N_HEADS * Q_BLOCK), F32)],
        compiler_params=_params(("parallel", "arbitrary")),
        name="nsa_prompt",
    )(qt, gtt, kc, vct, ks, vst, kw, vwt, imp_t)


def _nsa_sample_cmp_kernel(q_ref, kc_ref, vc_ref, imp_ref, oc_ref, idx_ref, *, qpos, n_sel):
    nc = kc_ref.shape[0]
    n_cand = imp_ref.shape[1]
    q = q_ref[...]
    head = lax.broadcasted_iota(jnp.int32, (N_HEADS, 1), 0)
    slope = _slope_column()
    c_end = lax.broadcasted_iota(jnp.int32, (1, nc), 1) * CMP_STRIDE + (CMP_LEN - 1)
    dist = qpos - c_end
    mask = dist >= 0
    s = _dot_nt(q, kc_ref[...]) - slope * dist.astype(F32)
    p = _softmax_masked(s, mask)
    oc_ref[...] = _dot(p.astype(BF16), vc_ref[...])

    p_sum = jnp.zeros_like(p)
    for g in range(N_KV):
        grp = jnp.sum(p[g * GROUP:(g + 1) * GROUP], axis=0, keepdims=True)
        p_sum = jnp.where(head == g, grp, p_sum)
    p_slc = _dot_exact_rhs(p_sum, imp_ref[...])

    j_idx = lax.broadcasted_iota(jnp.int32, (1, n_cand), 1)
    cur = qpos // SEL_BLOCK
    forced = (j_idx == 0) | (j_idx == cur) | (j_idx == cur - 1)
    valid = (j_idx * SEL_BLOCK) <= qpos
    score = jnp.where(valid, jnp.where(forced, FORCE_SCORE, p_slc), NEG_INF)
    score = jnp.where(j_idx < n_sel, score, REMOVED)

    cand = lax.broadcasted_iota(jnp.int32, score.shape, 1).astype(F32)
    out_lane = lax.broadcasted_iota(jnp.int32, (N_HEADS, 128), 1)

    def body(k, carry):
        sc, out = carry
        m = jnp.max(sc, axis=1, keepdims=True)
        first = jnp.min(jnp.where(sc == m, cand, float(n_cand)), axis=1, keepdims=True)
        return (jnp.where(cand == first, REMOVED, sc),
                jnp.where(out_lane == k, first.astype(jnp.int32), out))

    _, out = lax.fori_loop(0, SEL_TOP, body, (score, jnp.zeros((N_HEADS, 128), jnp.int32)),
                           unroll=True)
    idx_ref[...] = out


def nsa_sample_cmp(q8, kc, vc, imp, *, qpos, n_sel):
    nb = q8.shape[0]
    nc = kc.shape[1]
    return pl.pallas_call(
        functools.partial(_nsa_sample_cmp_kernel, qpos=qpos, n_sel=n_sel),
        grid=(nb,),
        in_specs=[
            pl.BlockSpec((None, N_HEADS, SLAB), lambda b: (b, 0, 0)),
            pl.BlockSpec((None, nc, SLAB), lambda b: (b, 0, 0)),
            pl.BlockSpec((None, nc, SLAB), lambda b: (b, 0, 0)),
            pl.BlockSpec(imp.shape, lambda b: (0, 0)),
        ],
        out_specs=[pl.BlockSpec((None, N_HEADS, SLAB), lambda b: (b, 0, 0)),
                   pl.BlockSpec((None, N_HEADS, 128), lambda b: (b, 0, 0))],
        out_shape=[jax.ShapeDtypeStruct((nb, N_HEADS, SLAB), F32),
                   jax.ShapeDtypeStruct((nb, N_HEADS, 128), jnp.int32)],
        compiler_params=_params(("parallel",)),
        name="nsa_sample_cmp",
    )(q8, kc, vc, imp)


def _nsa_sample_sel_kernel(pt_ref, idx_ref, *refs, qpos, n_past_blocks):
    del pt_ref
    n_pg = N_KV * SEL_TOP
    k_pages = refs[:n_pg]
    v_pages = refs[n_pg:2 * n_pg]
    (q_ref, gt_ref, oc_ref, kn_ref, vn_ref, wk_ref, wv_ref, wkn_ref, wvn_ref,
     o_ref, kr_ref, vr_ref) = refs[2 * n_pg:]
    b = pl.program_id(0)
    q = q_ref[...]
    head = lax.broadcasted_iota(jnp.int32, (N_HEADS, 1), 0)
    slope = _slope_column()
    grp_of_row = head >> 2
    assert GROUP == 4 and HEAD_DIM == 64 and SEL_BLOCK == 64
    lane_half = lax.broadcasted_iota(jnp.int32, (1, SLAB), 1) >> 6
    blocks_per_page = PAGE_SIZE // SEL_BLOCK
    n_keys = SEL_TOP * SEL_BLOCK
    key_slot = lax.broadcasted_iota(jnp.int32, (1, n_keys), 1) >> 6
    key_off = lax.broadcasted_iota(jnp.int32, (1, n_keys), 1) & (SEL_BLOCK - 1)
    qf = q.astype(F32)

    for i in range(n_pg):
        kr_ref[i * PAGE_SIZE:(i + 1) * PAGE_SIZE, :] = k_pages[i][...].T
        vr_ref[i * PAGE_SIZE:(i + 1) * PAGE_SIZE, :] = v_pages[i][...].T

    s_new = jnp.sum(qf * kn_ref[...], axis=1, keepdims=True)

    o_sel = jnp.zeros((N_HEADS, SLAB), F32)
    for g in range(N_KV):
        ks, vs = [], []
        start = jnp.zeros((1, n_keys), jnp.int32)
        for k in range(SEL_TOP):
            i = g * SEL_TOP + k
            j = idx_ref[b, i]
            off = pl.multiple_of(i * PAGE_SIZE + (j % blocks_per_page) * SEL_BLOCK, SEL_BLOCK)
            ks.append(kr_ref[pl.ds(off, SEL_BLOCK), :])
            vs.append(vr_ref[pl.ds(off, SEL_BLOCK), :])
            start = jnp.where(key_slot == k, j * SEL_BLOCK, start)
        k_all = jnp.concatenate(ks, axis=0).astype(BF16)
        v_all = jnp.concatenate(vs, axis=0).astype(BF16)
        spos = start + key_off
        dist = qpos - spos
        mask = (dist >= 0) & (start < n_past_blocks * SEL_BLOCK)
        s = _dot_nt(q, k_all) - slope * dist.astype(F32)
        s = jnp.where(mask, s, NEG_INF)
        m = jnp.maximum(jnp.max(s, axis=-1, keepdims=True), s_new)
        e = jnp.exp(s - m)
        e_new = jnp.exp(s_new - m)
        l = jnp.sum(e, axis=-1, keepdims=True) + e_new
        o_g = (_dot(e.astype(BF16), v_all) + e_new * vn_ref[...]) / l
        o_sel = jnp.where(grp_of_row == g, o_g, o_sel)

    n_win = wk_ref.shape[0]
    dist_w = n_win - lax.broadcasted_iota(jnp.int32, (1, n_win), 1)
    mask_w = (dist_w >= 0) & (dist_w < WINDOW)
    s_w = _dot_nt(q, wk_ref[...].astype(BF16)) - slope * dist_w.astype(F32)
    s_w = jnp.where(mask_w, s_w, NEG_INF)
    s_wn = jnp.sum(qf * wkn_ref[...], axis=1, keepdims=True)
    m_w = jnp.maximum(jnp.max(s_w, axis=-1, keepdims=True), s_wn)
    e_w = jnp.exp(s_w - m_w)
    e_wn = jnp.exp(s_wn - m_w)
    l_w = jnp.sum(e_w, axis=-1, keepdims=True) + e_wn
    o_win = (_dot(e_w.astype(BF16), wv_ref[...].astype(BF16)) + e_wn * wvn_ref[...]) / l_w

    gt = gt_ref[...]
    o = gt[:, 0:1] * oc_ref[...] + gt[:, 1:2] * o_sel + gt[:, 2:3] * o_win
    o_ref[...] = jnp.where(lane_half == grp_of_row, o, 0.0).astype(BF16)


def nsa_sample_sel(page_table, idx, pool_k, pool_v, q8, gt8, o_cmp, k_new, v_new,
                   win_k, win_v, wk_new, wv_new, *, qpos):
    nb, n_pages = page_table.shape
    n_win = win_k.shape[1]
    n_phys = pool_k.shape[0]
    blocks_per_page = PAGE_SIZE // SEL_BLOCK

    def page_spec(i):
        def index_map(b, pt, ix):
            page = jnp.clip(ix[b, i] // blocks_per_page, 0, n_pages - 1)
            return (jnp.clip(pt[b, page], 0, n_phys - 1), 0, 0)
        return pl.BlockSpec((None, SLAB, PAGE_SIZE), index_map)

    per_b = lambda r: pl.BlockSpec((None, r, SLAB), lambda b, pt, ix: (b, 0, 0))
    n_pg = N_KV * SEL_TOP
    grid_spec = pltpu.PrefetchScalarGridSpec(
        num_scalar_prefetch=2,
        grid=(nb,),
        in_specs=[page_spec(i) for i in range(n_pg)] * 2
        + [per_b(N_HEADS), per_b(N_HEADS), per_b(N_HEADS), per_b(1), per_b(1),
           per_b(n_win), per_b(n_win), per_b(1), per_b(1)],
        out_specs=per_b(N_HEADS),
        scratch_shapes=[pltpu.VMEM((n_pg * PAGE_SIZE, SLAB), F32)] * 2,
    )
    return pl.pallas_call(
        functools.partial(_nsa_sample_sel_kernel, qpos=qpos, n_past_blocks=n_pages * blocks_per_page),
        grid_spec=grid_spec,
        out_shape=jax.ShapeDtypeStruct((nb, N_HEADS, SLAB), BF16),
        compiler_params=_params(("arbitrary",)),
        name="nsa_sample_sel",
    )(page_table, idx, *([pool_k] * n_pg), *([pool_v] * n_pg), q8, gt8, o_cmp, k_new, v_new,
      win_k, win_v, wk_new, wv_new)


def _merge_ln_kernel(*refs, chunked):
    if chunked:
        (h_ref, oa_ref, uv_ref, sg_ref, wa_ref, wb_ref, wo_ref, g2_ref, b2_ref, lg_ref, lb_ref,
         sw_ref, sb_ref, o_ref) = refs
    else:
        (h_ref, oa_ref, uv_ref, sg_ref, wa_ref, wb_ref, wo_ref, g2_ref, b2_ref, lg_ref, lb_ref,
         sw_ref, sb_ref, o_ref, vn_ref) = refs
    tm = h_ref.shape[0]
    u = uv_ref[:, 0:D_B]
    vn = _layer_norm(uv_ref[:, D_B:2 * D_B], lg_ref[...], lb_ref[...])
    if chunked:
        row = lax.broadcasted_iota(jnp.int32, (CHUNK, CHUNK), 0)
        col = lax.broadcasted_iota(jnp.int32, (CHUNK, CHUNK), 1)
        vnb = vn.astype(BF16)
        chunks = []
        for c in range(tm // CHUNK):
            parts = []
            for hg in range(N_GROUPS_B):
                w = jnp.where(col <= row, sw_ref[hg], 0.0).astype(BF16)
                parts.append(_dot(w, vnb[c * CHUNK:(c + 1) * CHUNK, hg * GROUP_CH:(hg + 1) * GROUP_CH]))
            chunks.append(jnp.concatenate(parts, axis=1) + sb_ref[...])
        s = jnp.concatenate(chunks, axis=0)
    else:
        s = vn * sw_ref[...] + sb_ref[...]
        vn_ref[...] = vn
    z = u * s
    branch_a = _dot(oa_ref[...], wa_ref[...])
    branch_b = _dot(z.astype(BF16), wb_ref[...])
    merged = (sg_ref[:, 0:D_MODEL].astype(F32) * branch_a
              + sg_ref[:, D_MODEL:2 * D_MODEL].astype(F32) * branch_b)
    y = ALPHA * h_ref[...] + _dot(merged.astype(BF16), wo_ref[...])
    o_ref[...] = _layer_norm(y, g2_ref[...], b2_ref[...])


def merge_ln(h, oa, uv, sgab, wa, wb, wo, g2, b2, lg, lb, sw, sb, *, tm, chunked):
    m, d = h.shape
    row = lambda n: pl.BlockSpec((tm, n), lambda i: (i, 0))
    full = lambda a: pl.BlockSpec(a.shape, lambda i: (0,) * a.ndim)
    vec = lambda a: a.reshape(1, -1)
    g2, b2, lg, lb = vec(g2), vec(b2), vec(lg), vec(lb)
    out_shape = [jax.ShapeDtypeStruct((m, d), F32)]
    out_specs = [row(d)]
    if not chunked:
        out_shape.append(jax.ShapeDtypeStruct((m, D_B), F32))
        out_specs.append(row(D_B))
    res = pl.pallas_call(
        functools.partial(_merge_ln_kernel, chunked=chunked),
        grid=(m // tm,),
        in_specs=[row(d), row(oa.shape[1]), row(2 * D_B), row(2 * D_MODEL), full(wa), full(wb), full(wo),
                  full(g2), full(b2), full(lg), full(lb), full(sw), full(sb)],
        out_specs=out_specs,
        out_shape=out_shape,
        compiler_params=_params(("parallel",)),
        name="merge_ln",
    )(h, oa, uv, sgab, wa, wb, wo, g2, b2, lg, lb, sw, sb)
    return res if not chunked else res[0]


def _group_mask():
    return (np.arange(N_HEADS)[:, None] // GROUP == np.arange(N_KV)[None, :]).astype(np.float32)


def _split_w_in(w_in):
    sizes = (Q_W,) + (KV_W,) * 6 + (GATE_W, D_B, D_B, D_MODEL, D_MODEL)
    cuts = np.cumsum(sizes)[:-1].tolist()
    q, kc, vc, ks, vs, kw, vw, g, u, v, ga, gb = jnp.split(w_in, cuts, axis=1)
    d = w_in.shape[0]
    gm = jnp.asarray(_group_mask())
    wq = (q.reshape(d, N_HEADS, 1, HEAD_DIM) * gm[None, :, :, None]).reshape(d, Q_EXP)
    wkv = jnp.concatenate([kc, vc, ks, vs, kw, vw], axis=1)
    wgt = jnp.pad(g, ((0, 0), (0, 128 - GATE_W)))
    wuv = jnp.concatenate([u, v], axis=1)
    wgab = jnp.concatenate([ga, gb], axis=1)
    return tuple(w.astype(BF16) for w in (wq, q * LOG2E, wkv, wgt, wuv, wgab))


def _expand_branch_a(w_a):
    gm = jnp.asarray(_group_mask())
    w = w_a.reshape(N_HEADS, 1, HEAD_DIM, w_a.shape[1]) * gm[:, :, None, None]
    return w.reshape(Q_EXP, w_a.shape[1]).astype(BF16)


def _cmp_weights(pe, w1, w2):
    n_sub = CMP_LEN // CMP_STRIDE
    w1r = w1.reshape(n_sub, CMP_STRIDE, HEAD_DIM, CMP_HIDDEN)
    eye = jnp.eye(N_KV, dtype=w1.dtype)
    w_ab = jnp.einsum('spdc,gh->pgdshc', w1r, eye).reshape(SUB_W, n_sub * N_KV * CMP_HIDDEN)
    w2x = jnp.einsum('cd,gh->gchd', w2, eye).reshape(N_KV * CMP_HIDDEN, SLAB)
    pe8 = jnp.pad(pe.reshape(1, -1), ((0, 7), (0, 0)))
    return w_ab.astype(BF16), pe8.astype(BF16), w1.astype(BF16), w2x.astype(BF16)


def _importance_matrix(n_cmp, n_cand):
    i = np.arange(n_cmp)[:, None]
    j = np.arange(n_cand)[None, :]
    a = (i >= CMP_RATIO * j - 1) & (i <= CMP_RATIO * j + CMP_RATIO - 1)
    return jnp.asarray(a.astype(np.float32)).astype(BF16)


def _compress_rows(rows, cw):
    w_ab, pe8, w1b, w2x = cw
    nb = rows.shape[0]
    ident = jnp.arange(nb, dtype=jnp.int32).reshape(nb, 1)
    return cmp_fin(cmp_ab(rows, ident, w_ab, pages_per_step=1), None, pe8, w1b, w2x)


def _compress_paged(pool, page_table, tail, cw, *, pages_per_step):
    w_ab, pe8, w1b, w2x = cw
    ab = cmp_ab_paged(pool, page_table, w_ab, pages_per_step=pages_per_step)
    one = jnp.zeros((1, 1), jnp.int32)
    ab_tail = cmp_ab(tail.reshape(1, -1, SUB_W), one, w_ab, pages_per_step=1)
    ab_tail = ab_tail.reshape(tail.shape[0], tail.shape[1], -1)
    return cmp_fin(ab, ab_tail, pe8, w1b, w2x)


def _position_minor(a):
    n, p = a.shape[0], a.shape[1]
    return jnp.transpose(a, (0, 2, 3, 1)).reshape(n, SLAB, p)


def _position_major(a_t):
    n, _, p = a_t.shape
    return jnp.transpose(a_t.reshape(n, N_KV, HEAD_DIM, p), (0, 3, 1, 2))


def kernel(x_prompt, x_sample, cache_cmp_k, cache_cmp_v, cache_sel_k, cache_sel_v, cache_win_k,
           cache_win_v, page_table, ffn1_w_in, ffn1_w_out, ln1_g, ln1_b, w_in, cmp_pe_k, cmp_w1_k,
           cmp_w2_k, cmp_pe_v, cmp_w1_v, cmp_w2_v, gmlp_ln_g, gmlp_ln_b, spatial_w, spatial_b,
           w_branch_a, w_branch_b, w_out, ln2_g, ln2_b, ffn2_w_in, ffn2_w_out, ln3_g, ln3_b):
    nb, t, d = x_prompt.shape
    db, n_new, _ = x_sample.shape
    assert n_new == 1 and t % SEL_TILE == 0
    n_pages = page_table.shape[1]
    past = n_pages * PAGE_SIZE
    n_phys = cache_cmp_k.shape[0]
    m = nb * t

    f1_in, f1_out = ffn1_w_in.astype(BF16), ffn1_w_out.astype(BF16)
    f2_in, f2_out = ffn2_w_in.astype(BF16), ffn2_w_out.astype(BF16)
    wq, wq_base2, *proj_rest = _split_w_in(w_in)
    wa = _expand_branch_a(w_branch_a)
    wb = w_branch_b.astype(BF16)
    wo = w_out.astype(BF16)
    cw_k = _cmp_weights(cmp_pe_k, cmp_w1_k, cmp_w2_k)
    cw_v = _cmp_weights(cmp_pe_v, cmp_w1_v, cmp_w2_v)

    h = ffn_ln(x_prompt.reshape(m, d), f1_in, f1_out, ln1_g, ln1_b, tm=1024)
    (qt, kct, vct, kst, vst, kwt, vwt, kcb, vcb, ksb, kwb, vstb, vwtb, gtt, uv,
     sgab) = mixer_proj_t(h, wq_base2, *proj_rest, nb=nb, tm=512)
    n_cmp = t // CMP_STRIDE
    kcc = _compress_rows(kcb.reshape(nb, n_cmp, SUB_W), cw_k)
    vcc = _compress_rows(vcb.reshape(nb, n_cmp, SUB_W), cw_v)
    imp_t = _importance_matrix(n_cmp, t // SEL_BLOCK).T
    o_a = nsa_prompt(qt, gtt, kcc, jnp.swapaxes(vcc, 1, 2), ksb.reshape(nb, t, SLAB), vstb,
                     kwb.reshape(nb, t, SLAB), vwtb, imp_t)
    sb_tile = jnp.repeat(spatial_b.T, GROUP_CH, axis=1)
    h2 = merge_ln(h, o_a.reshape(m, Q_W), uv, sgab, w_branch_a.astype(BF16), wb, wo, ln2_g, ln2_b,
                  gmlp_ln_g, gmlp_ln_b, spatial_w, sb_tile, tm=512, chunked=True)
    y_prompt = ffn_ln(h2, f2_in, f2_out, ln3_g, ln3_b, tm=1024).reshape(nb, t, d)
    wkeep = min(WINDOW, t)
    p_outs = tuple(_position_major(a) for a in
                   (kct, vct, kst, vst, kwt[:, :, t - wkeep:], vwt[:, :, t - wkeep:]))

    hs = ffn_ln(x_sample.reshape(db, d), f1_in, f1_out, ln1_g, ln1_b, tm=db)
    (qs, kcs, vcs, kss, vss, kws, vws, _, gts, uvs, sgabs) = mixer_proj(hs, wq, *proj_rest, tm=db)
    t_pad = -(-(past + n_new) // SEL_BLOCK) * SEL_BLOCK
    n_tail = (t_pad - past) // CMP_STRIDE
    tail_rows = 16

    def tail_of(new):
        flat = jnp.pad(new, ((0, 0), (0, tail_rows * SUB_W - SLAB)))
        return flat.reshape(db, tail_rows, SUB_W)

    assert n_tail <= tail_rows
    pps = min(64, n_pages)
    kcc_s = _compress_paged(_position_minor(cache_cmp_k), page_table, tail_of(kcs), cw_k,
                            pages_per_step=pps)
    vcc_s = _compress_paged(_position_minor(cache_cmp_v), page_table, tail_of(vcs), cw_v,
                            pages_per_step=pps)
    n_sel_s = t_pad // SEL_BLOCK
    n_cand = -(-n_sel_s // 128) * 128
    imp_s = _importance_matrix(kcc_s.shape[1], n_cand)
    q8 = qs.reshape(db, N_HEADS, SLAB)
    o_cmp_s, idx_s = nsa_sample_cmp(q8, kcc_s, vcc_s, imp_s, qpos=past, n_sel=n_sel_s)
    idx_flat = idx_s[:, :N_KV, :SEL_TOP].reshape(db, N_KV * SEL_TOP)
    gt8 = jnp.pad(gts[:, :GATE_W].reshape(db, N_HEADS, 3), ((0, 0), (0, 0), (0, 125)))
    row3 = lambda a: a.reshape(db, 1, SLAB)
    win3 = lambda a: a.reshape(db, -1, SLAB)
    o_as = nsa_sample_sel(page_table, idx_flat, _position_minor(cache_sel_k),
                          _position_minor(cache_sel_v),
                          q8, gt8, o_cmp_s, row3(kss), row3(vss), win3(cache_win_k),
                          win3(cache_win_v), row3(kws), row3(vws), qpos=past)
    sw_row = jnp.repeat(spatial_w[:, 0, 0], GROUP_CH).reshape(1, D_B)
    sb_row = jnp.repeat(spatial_b[:, 0], GROUP_CH).reshape(1, D_B)
    hs2, vn_s = merge_ln(hs, o_as.reshape(db, Q_EXP), uvs, sgabs, wa, wb, wo, ln2_g, ln2_b,
                         gmlp_ln_g, gmlp_ln_b, sw_row, sb_row, tm=db, chunked=False)
    y_sample = ffn_ln(hs2, f2_in, f2_out, ln3_g, ln3_b, tm=db).reshape(db, n_new, d)
    kv4s = lambda a: a.reshape(db, n_new, N_KV, HEAD_DIM)
    wbuf = cache_win_k.shape[1]
    s_win_k = jnp.concatenate([cache_win_k, kv4s(kws)], axis=1)[:, -wbuf:]
    s_win_v = jnp.concatenate([cache_win_v, kv4s(vws)], axis=1)[:, -wbuf:]

    return (y_prompt, y_sample) + p_outs + (kv4s(kcs), kv4s(vcs), kv4s(kss), kv4s(vss),
                                            s_win_k, s_win_v, vn_s.reshape(db, n_new, D_B))
```

```python
import functools

import numpy as np
import jax
import jax.numpy as jnp
from jax import lax
from jax.experimental import pallas as pl
from jax.experimental.pallas import tpu as pltpu

F32 = jnp.float32
BF16 = jnp.bfloat16

D_MODEL = 1024
N_HEADS = 8
HEAD_DIM = 64
N_KV = 2
GROUP = N_HEADS // N_KV
PAGE_SIZE = 128
CMP_LEN = 32
CMP_STRIDE = 16
CMP_HIDDEN = 128
SEL_BLOCK = 64
CMP_RATIO = SEL_BLOCK // CMP_STRIDE
SEL_TOP = 16
WINDOW = 512
Q_BLOCK = 128
FORCE_SCORE = 1e4
D_B = 512
N_GROUPS_B = 4
GROUP_CH = D_B // N_GROUPS_B
CHUNK = 128
D_FF = 2816
DEPTH = 1
ALPHA = (2.0 * DEPTH) ** 0.25
LN_EPS = 1e-5
NEG_INF = -1e30
REMOVED = -3e38
LOG2E = float(np.log2(np.e))
Q_W = N_HEADS * HEAD_DIM
KV_W = N_KV * HEAD_DIM
GATE_W = N_HEADS * 3
SLAB = N_KV * HEAD_DIM
Q_EXP = N_HEADS * SLAB
SUB_W = CMP_STRIDE * SLAB
VMEM_LIMIT = 40 * 1024 * 1024
SEL_TILE = 1024
CMP_PAGE_GROUPS = 4


def _params(sem, vmem=VMEM_LIMIT, flags=None):
    return pltpu.CompilerParams(dimension_semantics=sem, vmem_limit_bytes=vmem, flags=flags)


def _layer_norm(x, g, b):
    mu = jnp.mean(x, axis=-1, keepdims=True)
    xc = x - mu
    var = jnp.mean(xc * xc, axis=-1, keepdims=True)
    return xc * lax.rsqrt(var + LN_EPS) * g + b


def _dot(a, b):
    return jnp.dot(a, b, preferred_element_type=F32)


def _dot_nt(a, b):
    return lax.dot_general(a, b, (((1,), (1,)), ((), ())), preferred_element_type=F32)


def _split3(x):
    hi = x.astype(BF16)
    r1 = x - hi.astype(F32)
    mid = r1.astype(BF16)
    lo = (r1 - mid.astype(F32)).astype(BF16)
    return hi, mid, lo


def _dot_exact_rhs(x, a_bf16):
    hi, mid, lo = _split3(x)
    return _dot(hi, a_bf16) + _dot(mid, a_bf16) + _dot(lo, a_bf16)


def _ffn_ln_kernel(x_ref, wg_ref, wu_ref, wo_ref, g_ref, b_ref, o_ref, xb_ref, acc_ref, *, n_f):
    f = pl.program_id(1)

    @pl.when(f == 0)
    def _():
        xb_ref[...] = x_ref[...].astype(BF16)
        acc_ref[...] = jnp.zeros_like(acc_ref)

    xb = xb_ref[...]
    gate = _dot(xb, wg_ref[...])
    up = _dot(xb, wu_ref[...])
    hid = (gate * jax.nn.sigmoid(gate)) * up
    acc_ref[...] += _dot(hid.astype(BF16), wo_ref[...])

    @pl.when(f == n_f - 1)
    def _():
        y = ALPHA * x_ref[...] + 0.5 * acc_ref[...]
        o_ref[...] = _layer_norm(y, g_ref[...], b_ref[...])


def ffn_ln(x, w_in_b, w_out_b, g, b, *, tm, tf=256):
    m, d = x.shape
    d_ff = w_out_b.shape[0]
    n_f = d_ff // tf
    return pl.pallas_call(
        functools.partial(_ffn_ln_kernel, n_f=n_f),
        grid=(m // tm, n_f),
        in_specs=[
            pl.BlockSpec((tm, d), lambda i, f: (i, 0)),
            pl.BlockSpec((d, tf), lambda i, f: (0, f)),
            pl.BlockSpec((d, tf), lambda i, f: (0, f + n_f)),
            pl.BlockSpec((tf, d), lambda i, f: (f, 0)),
            pl.BlockSpec((1, d), lambda i, f: (0, 0)),
            pl.BlockSpec((1, d), lambda i, f: (0, 0)),
        ],
        out_specs=pl.BlockSpec((tm, d), lambda i, f: (i, 0)),
        out_shape=jax.ShapeDtypeStruct((m, d), F32),
        scratch_shapes=[pltpu.VMEM((tm, d), BF16), pltpu.VMEM((tm, d), F32)],
        compiler_params=_params(("parallel", "arbitrary")),
        name="ffn_ln",
    )(x, w_in_b, w_in_b, w_out_b, g.reshape(1, d), b.reshape(1, d))


def _mixer_proj_kernel(h_ref, wq_ref, wkv_ref, wgt_ref, wuv_ref, wgab_ref,
                       q_ref, kc_ref, vc_ref, ks_ref, vs_ref, kw_ref, vw_ref,
                       kvb_ref, gt_ref, uv_ref, sgab_ref):
    hb = h_ref[...].astype(BF16)
    q_ref[...] = (_dot(hb, wq_ref[...]) * (HEAD_DIM ** -0.5)).astype(BF16)
    kv = _dot(hb, wkv_ref[...])
    for j, ref in enumerate((kc_ref, vc_ref, ks_ref, vs_ref, kw_ref, vw_ref)):
        ref[...] = kv[:, j * SLAB:(j + 1) * SLAB]
    kvb_ref[...] = kv.astype(BF16)
    gt_ref[...] = jax.nn.sigmoid(_dot(hb, wgt_ref[...]))
    uv_ref[...] = jax.nn.gelu(_dot(hb, wuv_ref[...]))
    sgab_ref[...] = jax.nn.sigmoid(_dot(hb, wgab_ref[...])).astype(BF16)


def mixer_proj(h, wq, wkv, wgt, wuv, wgab, *, tm):
    m, d = h.shape
    row = lambda n: pl.BlockSpec((tm, n), lambda i: (i, 0))
    full = lambda w: pl.BlockSpec(w.shape, lambda i: (0, 0))
    out_shape = (
        [jax.ShapeDtypeStruct((m, Q_EXP), BF16)]
        + [jax.ShapeDtypeStruct((m, SLAB), F32)] * 6
        + [jax.ShapeDtypeStruct((m, 6 * SLAB), BF16),
           jax.ShapeDtypeStruct((m, 128), F32),
           jax.ShapeDtypeStruct((m, 2 * D_B), F32),
           jax.ShapeDtypeStruct((m, 2 * D_MODEL), BF16)])
    out_specs = ([row(Q_EXP)] + [row(SLAB)] * 6
                 + [row(6 * SLAB), row(128), row(2 * D_B), row(2 * D_MODEL)])
    return pl.pallas_call(
        _mixer_proj_kernel,
        grid=(m // tm,),
        in_specs=[row(d), full(wq), full(wkv), full(wgt), full(wuv), full(wgab)],
        out_specs=out_specs,
        out_shape=out_shape,
        compiler_params=_params(("parallel",)),
        name="mixer_proj",
    )(h, wq, wkv, wgt, wuv, wgab)


def _mixer_proj_t_kernel(h_ref, wq_ref, wkv_ref, wgt_ref, wuv_ref, wgab_ref,
                         qt_ref, kct_ref, vct_ref, kst_ref, vst_ref, kwt_ref, vwt_ref,
                         kcb_ref, vcb_ref, ksb_ref, kwb_ref, vstb_ref, vwtb_ref,
                         gtt_ref, uv_ref, sgab_ref):
    hb = h_ref[...].astype(BF16)
    q = _dot(hb, wq_ref[...]) * (HEAD_DIM ** -0.5)
    zeros = jnp.zeros((HEAD_DIM, h_ref.shape[0]), BF16)
    for pair in range(N_HEADS // 2):
        pair_t = q[:, pair * SLAB:(pair + 1) * SLAB].T.astype(BF16)
        for e in range(2):
            h = 2 * pair + e
            g = h // GROUP
            qt_ref[h * SLAB + g * HEAD_DIM:h * SLAB + (g + 1) * HEAD_DIM, :] = (
                pair_t[e * HEAD_DIM:(e + 1) * HEAD_DIM, :])
            qt_ref[h * SLAB + (1 - g) * HEAD_DIM:h * SLAB + (2 - g) * HEAD_DIM, :] = zeros
    kv = _dot(hb, wkv_ref[...])
    slabs = [kv[:, j * SLAB:(j + 1) * SLAB] for j in range(6)]
    for slab, ref in zip(slabs, (kct_ref, vct_ref, kst_ref, vst_ref, kwt_ref, vwt_ref)):
        ref[...] = slab.T
    kcb_ref[...] = slabs[0].astype(BF16)
    vcb_ref[...] = slabs[1].astype(BF16)
    ksb_ref[...] = slabs[2].astype(BF16)
    kwb_ref[...] = slabs[4].astype(BF16)
    vstb_ref[...] = slabs[3].T.astype(BF16)
    vwtb_ref[...] = slabs[5].T.astype(BF16)
    gtt_ref[...] = jax.nn.sigmoid(_dot(hb, wgt_ref[...])).T
    uv_ref[...] = jax.nn.gelu(_dot(hb, wuv_ref[...]))
    sgab_ref[...] = jax.nn.sigmoid(_dot(hb, wgab_ref[...])).astype(BF16)


def mixer_proj_t(h, wq, wkv, wgt, wuv, wgab, *, nb, tm):
    m, d = h.shape
    t = m // nb
    per_b = t // tm
    row = lambda n: pl.BlockSpec((tm, n), lambda i: (i, 0))
    col = lambda n: pl.BlockSpec((None, n, tm), lambda i: (i // per_b, 0, i % per_b))
    full = lambda w: pl.BlockSpec(w.shape, lambda i: (0, 0))
    sds = jax.ShapeDtypeStruct
    out_shape = ([sds((nb, Q_EXP, t), BF16)] + [sds((nb, SLAB, t), F32)] * 6
                 + [sds((m, SLAB), BF16)] * 4 + [sds((nb, SLAB, t), BF16)] * 2
                 + [sds((nb, 128, t), F32), sds((m, 2 * D_B), F32), sds((m, 2 * D_MODEL), BF16)])
    out_specs = ([col(Q_EXP)] + [col(SLAB)] * 6 + [row(SLAB)] * 4 + [col(SLAB)] * 2
                 + [col(128), row(2 * D_B), row(2 * D_MODEL)])
    return pl.pallas_call(
        _mixer_proj_t_kernel,
        grid=(m // tm,),
        in_specs=[row(d), full(wq), full(wkv), full(wgt), full(wuv), full(wgab)],
        out_specs=out_specs,
        out_shape=out_shape,
        compiler_params=_params(("parallel",)),
        name="mixer_proj_t",
    )(h, wq, wkv, wgt, wuv, wgab)


def _cmp_ab_kernel(pt_ref, *refs, n_pages):
    del pt_ref
    page_refs = refs[:n_pages]
    w_ref = refs[n_pages]
    o_ref = refs[n_pages + 1]
    if n_pages == 1:
        x = page_refs[0][...]
    else:
        x = jnp.concatenate([r[...] for r in page_refs], axis=0)
    o_ref[...] = _dot(x.astype(BF16), w_ref[...])


def cmp_ab(pool, page_table, w_ab, *, pages_per_step):
    nb, n_pages = page_table.shape
    rows = pool.shape[1]
    pp = pages_per_step
    n_steps = n_pages // pp

    def page_spec(k):
        return pl.BlockSpec((None, rows, SUB_W), lambda b, s, pt: (pt[b, s * pp + k], 0, 0))

    grid_spec = pltpu.PrefetchScalarGridSpec(
        num_scalar_prefetch=1,
        grid=(nb, n_steps),
        in_specs=[page_spec(k) for k in range(pp)]
        + [pl.BlockSpec(w_ab.shape, lambda b, s, pt: (0, 0))],
        out_specs=pl.BlockSpec((None, pp * rows, w_ab.shape[1]), lambda b, s, pt: (b, s, 0)),
    )
    return pl.pallas_call(
        functools.partial(_cmp_ab_kernel, n_pages=pp),
        grid_spec=grid_spec,
        out_shape=jax.ShapeDtypeStruct((nb, n_pages * rows, w_ab.shape[1]), F32),
        compiler_params=_params(("parallel", "arbitrary")),
        name="cmp_ab",
    )(page_table, *([pool] * pp), w_ab)


def _cmp_ab_paged_kernel(pt_ref, *refs, n_pages):
    del pt_ref
    page_refs = refs[:n_pages]
    w_ref, o_ref, rows_ref = refs[n_pages:]
    per_group = n_pages // CMP_PAGE_GROUPS
    n_sub = per_group * PAGE_SIZE // CMP_STRIDE
    for grp in range(CMP_PAGE_GROUPS):
        base = grp * per_group * PAGE_SIZE
        for k in range(per_group):
            rows_ref[base + k * PAGE_SIZE:base + (k + 1) * PAGE_SIZE, :] = (
                page_refs[grp * per_group + k][...].T)
        x = jnp.concatenate([rows_ref[pl.ds(base + p, n_sub, stride=CMP_STRIDE), :].astype(BF16)
                             for p in range(CMP_STRIDE)], axis=1)
        o_ref[grp * n_sub:(grp + 1) * n_sub, :] = _dot(x, w_ref[...])


def cmp_ab_paged(pool, page_table, w_ab, *, pages_per_step):
    nb, n_pages = page_table.shape
    pp = pages_per_step
    sub_per_page = PAGE_SIZE // CMP_STRIDE

    n_phys = pool.shape[0]

    def page_spec(k):
        return pl.BlockSpec((None, SLAB, PAGE_SIZE),
                            lambda b, s, pt: (jnp.clip(pt[b, s * pp + k], 0, n_phys - 1), 0, 0))

    grid_spec = pltpu.PrefetchScalarGridSpec(
        num_scalar_prefetch=1,
        grid=(nb, n_pages // pp),
        in_specs=[page_spec(k) for k in range(pp)]
        + [pl.BlockSpec(w_ab.shape, lambda b, s, pt: (0, 0))],
        out_specs=pl.BlockSpec((None, pp * sub_per_page, w_ab.shape[1]), lambda b, s, pt: (b, s, 0)),
        scratch_shapes=[pltpu.VMEM((pp * PAGE_SIZE, SLAB), F32)],
    )
    return pl.pallas_call(
        functools.partial(_cmp_ab_paged_kernel, n_pages=pp),
        grid_spec=grid_spec,
        out_shape=jax.ShapeDtypeStruct((nb, n_pages * sub_per_page, w_ab.shape[1]), F32),
        compiler_params=_params(("parallel", "arbitrary")),
        name="cmp_ab_paged",
    )(page_table, *([pool] * pp), w_ab)


def _cmp_fin_kernel(*refs, has_tail):
    if has_tail:
        ab_ref, tail_ref, pe_ref, w1_ref, w2_ref, o_ref = refs
    else:
        ab_ref, pe_ref, w1_ref, w2_ref, o_ref = refs
    hw = 2 * CMP_HIDDEN
    n = ab_ref.shape[0]
    pe_h = _dot(pe_ref[...], w1_ref[...])[0:1, :]
    bias = jnp.concatenate([pe_h, pe_h], axis=1)
    w2 = w2_ref[...]
    first = ab_ref[:, 0:hw]
    second = pltpu.roll(ab_ref[:, hw:2 * hw], n - 1, 0)
    if has_tail:
        t_first = tail_ref[:, 0:hw]
        t_second = tail_ref[:, hw:2 * hw]
        nt = tail_ref.shape[0]
        is_last = lax.broadcasted_iota(jnp.int32, (n, 1), 0) == n - 1
        second = jnp.where(is_last, t_second[0:1, :], second)
        t_hid = t_first + pltpu.roll(t_second, nt - 1, 0) + bias
        o_ref[n:n + nt, :] = _dot(jax.nn.gelu(t_hid).astype(BF16), w2).astype(BF16)
        n_out = o_ref.shape[0]
        if n_out > n + nt:
            o_ref[n + nt:n_out, :] = jnp.zeros((n_out - n - nt, SLAB), BF16)
    hid = first + second + bias
    o_ref[0:n, :] = _dot(jax.nn.gelu(hid).astype(BF16), w2).astype(BF16)


def cmp_fin(ab, tail, pe8, w1b, w2x):
    nb, n, w = ab.shape
    has_tail = tail is not None
    nt = tail.shape[1] if has_tail else 0
    full = lambda a: pl.BlockSpec(a.shape, lambda b: (0, 0))
    in_specs = [pl.BlockSpec((None, n, w), lambda b: (b, 0, 0))]
    args = [ab]
    if has_tail:
        in_specs.append(pl.BlockSpec((None, nt, w), lambda b: (b, 0, 0)))
        args.append(tail)
    in_specs += [full(pe8), full(w1b), full(w2x)]
    args += [pe8, w1b, w2x]
    n_out = -(-(n + nt) // 128) * 128 if has_tail else n
    return pl.pallas_call(
        functools.partial(_cmp_fin_kernel, has_tail=has_tail),
        grid=(nb,),
        in_specs=in_specs,
        out_specs=pl.BlockSpec((None, n_out, SLAB), lambda b: (b, 0, 0)),
        out_shape=jax.ShapeDtypeStruct((nb, n_out, SLAB), BF16),
        compiler_params=_params(("parallel",)),
        name="cmp_fin",
    )(*args)


def _slope_of_head(h):
    return 2.0 ** (-(h + 1))


def _slope_column():
    head = lax.broadcasted_iota(jnp.int32, (N_HEADS, 1), 0)
    slope = jnp.zeros((N_HEADS, 1), F32)
    for h in range(N_HEADS):
        slope = jnp.where(head == h, _slope_of_head(h), slope)
    return slope


def _top_k_mask_t(score_t, k):
    n = score_t.shape[0]
    cand_idx = lax.broadcasted_iota(jnp.int32, score_t.shape, 0).astype(F32)

    def body(_, carry):
        s, sel = carry
        m = jnp.max(s, axis=0, keepdims=True)
        first = jnp.min(jnp.where(s == m, cand_idx, float(n)), axis=0, keepdims=True)
        pick = cand_idx == first
        return jnp.where(pick, REMOVED, s), jnp.where(pick, 1.0, sel)

    _, sel = lax.fori_loop(0, k, body, (score_t, jnp.zeros_like(score_t)), unroll=True)
    return sel


def _softmax_masked(s, mask):
    s = jnp.where(mask, s, NEG_INF)
    m = jnp.max(s, axis=-1, keepdims=True)
    e = jnp.exp(s - m)
    return jnp.where(mask, e / jnp.sum(e, axis=-1, keepdims=True), 0.0)


def _softmax_masked_t(s, mask):
    s = jnp.where(mask, s, NEG_INF)
    m = jnp.max(s, axis=0, keepdims=True)
    e = jnp.exp2(s - m)
    return jnp.where(mask, e * (1.0 / jnp.sum(e, axis=0, keepdims=True)), 0.0)


def _nsa_prompt_kernel(qt_ref, gtt_ref, kc_ref, vct_ref, ks_ref, vst_ref, kw_ref, vwt_ref, impt_ref,
                       o_ref, selt_ref, bias_c_ref, bias_w_ref, bias_s_ref):
    qb = pl.program_id(1)
    s0 = qb * Q_BLOCK
    nq = Q_BLOCK
    hq = N_HEADS * nq
    nc = kc_ref.shape[0]
    n_sel = ks_ref.shape[0] // SEL_BLOCK
    tk = SEL_TILE
    assert HEAD_DIM == 64 and SEL_BLOCK == 64 and Q_BLOCK == 128

    lane = lax.broadcasted_iota(jnp.int32, (1, hq), 1)
    q_loc = lane & (nq - 1)
    qpos_f = (s0 + q_loc).astype(F32)
    qpos_q = s0 + lax.broadcasted_iota(jnp.int32, (1, nq), 1)
    slope = jnp.zeros((1, hq), F32)
    for h in range(N_HEADS):
        slope = jnp.where((lane >> 7) == h, _slope_of_head(h) * LOG2E, slope)
    n_win = WINDOW + nq
    k_loc = lax.broadcasted_iota(jnp.int32, (tk, 1), 0)

    @pl.when(qb == 0)
    def _():
        q_loc_f = q_loc.astype(F32)
        c_end = lax.broadcasted_iota(jnp.int32, (nc, 1), 0) * CMP_STRIDE + (CMP_LEN - 1)
        bias_c_ref[...] = slope * (q_loc_f - c_end.astype(F32))
        w_loc = lax.broadcasted_iota(jnp.int32, (n_win, 1), 0)
        bias_w_ref[...] = slope * (q_loc_f - w_loc.astype(F32))
        bias_s_ref[...] = slope * k_loc.astype(F32)

    j_idx = lax.broadcasted_iota(jnp.int32, (n_sel, 1), 0)
    cur = qpos_q >> 6
    forced = (j_idx == 0) | (j_idx == cur) | (j_idx == cur - 1)
    valid = (j_idx * SEL_BLOCK) <= qpos_q

    w0 = pl.multiple_of(jnp.maximum(s0 - WINDOW, 0), Q_BLOCK)
    kw = kw_ref[pl.ds(w0, n_win), :]
    vwt = vwt_ref[:, pl.ds(w0, n_win)]

    assert tk % nq == 0
    n_full = s0 // tk

    qt = jnp.concatenate([qt_ref[h * SLAB:(h + 1) * SLAB, :] for h in range(N_HEADS)],
                         axis=1)

    visible = slope * (-s0).astype(F32)

    def cmp_probs(n):
        bias_c = bias_c_ref[0:n, :]
        p_n = _softmax_masked_t(_dot(kc_ref[0:n, :], qt) - bias_c, bias_c >= visible)
        return p_n if n == nc else jnp.concatenate([p_n, jnp.zeros((nc - n, hq), F32)], axis=0)

    early = s0 + nq <= (nc // 2) * CMP_STRIDE + CMP_LEN - 1
    p = lax.cond(early, lambda: cmp_probs(nc // 2), lambda: cmp_probs(nc))
    o_cmp = _dot(vct_ref[...], p.astype(BF16))

    imp_t = impt_ref[...]
    scores = []
    for g in range(N_KV):
        p_sum = p[:, g * GROUP * nq:(g * GROUP + 1) * nq]
        for r in range(1, GROUP):
            p_sum = p_sum + p[:, (g * GROUP + r) * nq:(g * GROUP + r + 1) * nq]
        hi, mid, lo = _split3(p_sum)
        p_slc = _dot(imp_t, hi) + _dot(imp_t, mid) + _dot(imp_t, lo)
        scores.append(jnp.where(valid, jnp.where(forced, FORCE_SCORE, p_slc), NEG_INF))
    sel = _top_k_mask_t(jnp.concatenate(scores, axis=1), min(SEL_TOP, n_sel))
    for g in range(N_KV):
        selt_ref[g] = sel[:, g * nq:(g + 1) * nq]


    def sel_tile(kt, rows, carry, diagonal):
        m_old, l_old, acc_old = carry
        k = ks_ref[pl.ds(kt, rows), :]
        vt = vst_ref[:, pl.ds(kt, rows)]
        s_rel = _dot(k, qt) + bias_s_ref[0:rows, :]
        shift = slope * (kt.astype(F32) - qpos_f)
        cols = []
        for g in range(N_KV):
            picked = jnp.concatenate(
                [jnp.broadcast_to(selt_ref[g, pl.ds((kt >> 6) + jj, 1), :], (SEL_BLOCK, nq))
                 for jj in range(rows // SEL_BLOCK)], axis=0) > 0.5
            if diagonal:
                picked = picked & ((kt + k_loc[0:rows]) <= qpos_q)
            for r in range(GROUP):
                h = g * GROUP + r
                cols.append(jnp.where(picked, s_rel[:, h * nq:(h + 1) * nq], NEG_INF))
        s_rel = jnp.concatenate(cols, axis=1)
        m_new = jnp.maximum(m_old, jnp.max(s_rel, axis=0, keepdims=True) + shift)
        alpha = jnp.exp2(m_old - m_new)
        p_t = jnp.exp2(s_rel - (m_new - shift))
        l_new = alpha * l_old + jnp.sum(p_t, axis=0, keepdims=True)
        acc_new = alpha * acc_old + _dot(vt, p_t.astype(BF16))
        return m_new, l_new, acc_new

    init = (jnp.full((1, hq), NEG_INF, F32), jnp.zeros((1, hq), F32), jnp.zeros((SLAB, hq), F32))
    state = lax.fori_loop(
        0, n_full,
        lambda t, st: sel_tile(pl.multiple_of(t * tk, tk), tk, st, diagonal=False), init)
    half = tk // 2
    assert half % nq == 0
    base = pl.multiple_of(n_full * tk, tk)
    second = (s0 - base) >= half
    state = lax.cond(second, lambda st: sel_tile(base, half, st, diagonal=False), lambda st: st,
                     state)
    diag = pl.multiple_of(base + second.astype(jnp.int32) * half, half)
    _, l_sel, acc_sel = sel_tile(diag, half, state, diagonal=True)
    o_sel = acc_sel * (1.0 / l_sel)

    bias_w = bias_w_ref[...]
    lead = (s0 - w0).astype(F32)
    mask_w = (bias_w >= slope * (-lead)) & (bias_w < slope * (WINDOW - lead))
    o_win = _dot(vwt, _softmax_masked_t(_dot(kw, qt) - bias_w, mask_w).astype(BF16))

    for pair in range(N_HEADS // 2):
        halves = []
        for h in (2 * pair, 2 * pair + 1):
            cols = slice(h * nq, (h + 1) * nq)
            rows = slice((h // GROUP) * HEAD_DIM, (h // GROUP + 1) * HEAD_DIM)
            halves.append(gtt_ref[3 * h:3 * h + 1, :] * o_cmp[rows, cols]
                          + gtt_ref[3 * h + 1:3 * h + 2, :] * o_sel[rows, cols]
                          + gtt_ref[3 * h + 2:3 * h + 3, :] * o_win[rows, cols])
        o_ref[:, pair * SLAB:(pair + 1) * SLAB] = jnp.concatenate(halves, axis=0).T.astype(BF16)


def nsa_prompt(qt, gtt, kc, vct, ks, vst, kw, vwt, imp_t):
    nb, _, t = qt.shape
    nc = kc.shape[1]
    rows = pl.BlockSpec((None, t, SLAB), lambda b, i: (b, 0, 0))
    cols = pl.BlockSpec((None, SLAB, t), lambda b, i: (b, 0, 0))
    return pl.pallas_call(
        _nsa_prompt_kernel,
        grid=(nb, t // Q_BLOCK),
        in_specs=[
            pl.BlockSpec((None, Q_EXP, Q_BLOCK), lambda b, i: (b, 0, i)),
            pl.BlockSpec((None, 128, Q_BLOCK), lambda b, i: (b, 0, i)),
            pl.BlockSpec((None, nc, SLAB), lambda b, i: (b, 0, 0)),
            pl.BlockSpec((None, SLAB, nc), lambda b, i: (b, 0, 0)),
            rows, cols, rows, cols,
            pl.BlockSpec(imp_t.shape, lambda b, i: (0, 0)),
        ],
        out_specs=pl.BlockSpec((None, Q_BLOCK, Q_W), lambda b, i: (b, i, 0)),
        out_shape=jax.ShapeDtypeStruct((nb, t, Q_W), BF16),
        scratch_shapes=[pltpu.VMEM((N_KV, t // SEL_BLOCK, Q_BLOCK), F32),
                        pltpu.VMEM((nc, N_HEADS * Q_BLOCK), F32),
                        pltpu.VMEM((WINDOW + Q_BLOCK, N_HEADS * Q_BLOCK), F32),
                        pltpu.VMEM((SEL_TILE, N_HEADS * Q_BLOCK), F32)],
        compiler_params=_params(("parallel", "arbitrary")),
        name="nsa_prompt",
    )(qt, gtt, kc, vct, ks, vst, kw, vwt, imp_t)


def _nsa_sample_cmp_kernel(q_ref, kc_ref, vc_ref, imp_ref, oc_ref, idx_ref, *, qpos, n_sel):
    nc = kc_ref.shape[0]
    n_cand = imp_ref.shape[1]
    q = q_ref[...]
    head = lax.broadcasted_iota(jnp.int32, (N_HEADS, 1), 0)
    slope = _slope_column()
    c_end = lax.broadcasted_iota(jnp.int32, (1, nc), 1) * CMP_STRIDE + (CMP_LEN - 1)
    dist = qpos - c_end
    mask = dist >= 0
    s = _dot_nt(q, kc_ref[...]) - slope * dist.astype(F32)
    p = _softmax_masked(s, mask)
    oc_ref[...] = _dot(p.astype(BF16), vc_ref[...])

    p_sum = jnp.zeros_like(p)
    for g in range(N_KV):
        grp = jnp.sum(p[g * GROUP:(g + 1) * GROUP], axis=0, keepdims=True)
        p_sum = jnp.where(head == g, grp, p_sum)
    p_slc = _dot_exact_rhs(p_sum, imp_ref[...])

    j_idx = lax.broadcasted_iota(jnp.int32, (1, n_cand), 1)
    cur = qpos // SEL_BLOCK
    forced = (j_idx == 0) | (j_idx == cur) | (j_idx == cur - 1)
    valid = (j_idx * SEL_BLOCK) <= qpos
    score = jnp.where(valid, jnp.where(forced, FORCE_SCORE, p_slc), NEG_INF)
    score = jnp.where(j_idx < n_sel, score, REMOVED)

    cand = lax.broadcasted_iota(jnp.int32, score.shape, 1).astype(F32)
    out_lane = lax.broadcasted_iota(jnp.int32, (N_HEADS, 128), 1)

    def body(k, carry):
        sc, out = carry
        m = jnp.max(sc, axis=1, keepdims=True)
        first = jnp.min(jnp.where(sc == m, cand, float(n_cand)), axis=1, keepdims=True)
        return (jnp.where(cand == first, REMOVED, sc),
                jnp.where(out_lane == k, first.astype(jnp.int32), out))

    _, out = lax.fori_loop(0, SEL_TOP, body, (score, jnp.zeros((N_HEADS, 128), jnp.int32)),
                           unroll=True)
    idx_ref[...] = out


def nsa_sample_cmp(q8, kc, vc, imp, *, qpos, n_sel):
    nb = q8.shape[0]
    nc = kc.shape[1]
    return pl.pallas_call(
        functools.partial(_nsa_sample_cmp_kernel, qpos=qpos, n_sel=n_sel),
        grid=(nb,),
        in_specs=[
            pl.BlockSpec((None, N_HEADS, SLAB), lambda b: (b, 0, 0)),
            pl.BlockSpec((None, nc, SLAB), lambda b: (b, 0, 0)),
            pl.BlockSpec((None, nc, SLAB), lambda b: (b, 0, 0)),
            pl.BlockSpec(imp.shape, lambda b: (0, 0)),
        ],
        out_specs=[pl.BlockSpec((None, N_HEADS, SLAB), lambda b: (b, 0, 0)),
                   pl.BlockSpec((None, N_HEADS, 128), lambda b: (b, 0, 0))],
        out_shape=[jax.ShapeDtypeStruct((nb, N_HEADS, SLAB), F32),
                   jax.ShapeDtypeStruct((nb, N_HEADS, 128), jnp.int32)],
        compiler_params=_params(("parallel",)),
        name="nsa_sample_cmp",
    )(q8, kc, vc, imp)


def _nsa_sample_sel_kernel(pt_ref, idx_ref, *refs, qpos, n_past_blocks):
    del pt_ref
    n_pg = N_KV * SEL_TOP
    k_pages = refs[:n_pg]
    v_pages = refs[n_pg:2 * n_pg]
    (q_ref, gt_ref, oc_ref, kn_ref, vn_ref, wk_ref, wv_ref, wkn_ref, wvn_ref,
     o_ref, kr_ref, vr_ref) = refs[2 * n_pg:]
    b = pl.program_id(0)
    q = q_ref[...]
    head = lax.broadcasted_iota(jnp.int32, (N_HEADS, 1), 0)
    slope = _slope_column()
    grp_of_row = head >> 2
    assert GROUP == 4 and HEAD_DIM == 64 and SEL_BLOCK == 64
    lane_half = lax.broadcasted_iota(jnp.int32, (1, SLAB), 1) >> 6
    blocks_per_page = PAGE_SIZE // SEL_BLOCK
    n_keys = SEL_TOP * SEL_BLOCK
    key_slot = lax.broadcasted_iota(jnp.int32, (1, n_keys), 1) >> 6
    key_off = lax.broadcasted_iota(jnp.int32, (1, n_keys), 1) & (SEL_BLOCK - 1)
    qf = q.astype(F32)

    for i in range(n_pg):
        kr_ref[i * PAGE_SIZE:(i + 1) * PAGE_SIZE, :] = k_pages[i][...].T
        vr_ref[i * PAGE_SIZE:(i + 1) * PAGE_SIZE, :] = v_pages[i][...].T

    s_new = jnp.sum(qf * kn_ref[...], axis=1, keepdims=True)

    o_sel = jnp.zeros((N_HEADS, SLAB), F32)
    for g in range(N_KV):
        ks, vs = [], []
        start = jnp.zeros((1, n_keys), jnp.int32)
        for k in range(SEL_TOP):
            i = g * SEL_TOP + k
            j = idx_ref[b, i]
            off = pl.multiple_of(i * PAGE_SIZE + (j % blocks_per_page) * SEL_BLOCK, SEL_BLOCK)
            ks.append(kr_ref[pl.ds(off, SEL_BLOCK), :])
            vs.append(vr_ref[pl.ds(off, SEL_BLOCK), :])
            start = jnp.where(key_slot == k, j * SEL_BLOCK, start)
        k_all = jnp.concatenate(ks, axis=0).astype(BF16)
        v_all = jnp.concatenate(vs, axis=0).astype(BF16)
        spos = start + key_off
        dist = qpos - spos
        mask = (dist >= 0) & (start < n_past_blocks * SEL_BLOCK)
        s = _dot_nt(q, k_all) - slope * dist.astype(F32)
        s = jnp.where(mask, s, NEG_INF)
        m = jnp.maximum(jnp.max(s, axis=-1, keepdims=True), s_new)
        e = jnp.exp(s - m)
        e_new = jnp.exp(s_new - m)
        l = jnp.sum(e, axis=-1, keepdims=True) + e_new
        o_g = (_dot(e.astype(BF16), v_all) + e_new * vn_ref[...]) / l
        o_sel = jnp.where(grp_of_row == g, o_g, o_sel)

    n_win = wk_ref.shape[0]
    dist_w = n_win - lax.broadcasted_iota(jnp.int32, (1, n_win), 1)
    mask_w = (dist_w >= 0) & (dist_w < WINDOW)
    s_w = _dot_nt(q, wk_ref[...].astype(BF16)) - slope * dist_w.astype(F32)
    s_w = jnp.where(mask_w, s_w, NEG_INF)
    s_wn = jnp.sum(qf * wkn_ref[...], axis=1, keepdims=True)
    m_w = jnp.maximum(jnp.max(s_w, axis=-1, keepdims=True), s_wn)
    e_w = jnp.exp(s_w - m_w)
    e_wn = jnp.exp(s_wn - m_w)
    l_w = jnp.sum(e_w, axis=-1, keepdims=True) + e_wn
    o_win = (_dot(e_w.astype(BF16), wv_ref[...].astype(BF16)) + e_wn * wvn_ref[...]) / l_w

    gt = gt_ref[...]
    o = gt[:, 0:1] * oc_ref[...] + gt[:, 1:2] * o_sel + gt[:, 2:3] * o_win
    o_ref[...] = jnp.where(lane_half == grp_of_row, o, 0.0).astype(BF16)


def nsa_sample_sel(page_table, idx, pool_k, pool_v, q8, gt8, o_cmp, k_new, v_new,
                   win_k, win_v, wk_new, wv_new, *, qpos):
    nb, n_pages = page_table.shape
    n_win = win_k.shape[1]
    n_phys = pool_k.shape[0]
    blocks_per_page = PAGE_SIZE // SEL_BLOCK

    def page_spec(i):
        def index_map(b, pt, ix):
            page = jnp.clip(ix[b, i] // blocks_per_page, 0, n_pages - 1)
            return (jnp.clip(pt[b, page], 0, n_phys - 1), 0, 0)
        return pl.BlockSpec((None, SLAB, PAGE_SIZE), index_map)

    per_b = lambda r: pl.BlockSpec((None, r, SLAB), lambda b, pt, ix: (b, 0, 0))
    n_pg = N_KV * SEL_TOP
    grid_spec = pltpu.PrefetchScalarGridSpec(
        num_scalar_prefetch=2,
        grid=(nb,),
        in_specs=[page_spec(i) for i in range(n_pg)] * 2
        + [per_b(N_HEADS), per_b(N_HEADS), per_b(N_HEADS), per_b(1), per_b(1),
           per_b(n_win), per_b(n_win), per_b(1), per_b(1)],
        out_specs=per_b(N_HEADS),
        scratch_shapes=[pltpu.VMEM((n_pg * PAGE_SIZE, SLAB), F32)] * 2,
    )
    return pl.pallas_call(
        functools.partial(_nsa_sample_sel_kernel, qpos=qpos, n_past_blocks=n_pages * blocks_per_page),
        grid_spec=grid_spec,
        out_shape=jax.ShapeDtypeStruct((nb, N_HEADS, SLAB), BF16),
        compiler_params=_params(("arbitrary",)),
        name="nsa_sample_sel",
    )(page_table, idx, *([pool_k] * n_pg), *([pool_v] * n_pg), q8, gt8, o_cmp, k_new, v_new,
      win_k, win_v, wk_new, wv_new)


def _merge_ln_kernel(*refs, chunked):
    if chunked:
        (h_ref, oa_ref, uv_ref, sg_ref, wa_ref, wb_ref, wo_ref, g2_ref, b2_ref, lg_ref, lb_ref,
         sw_ref, sb_ref, o_ref) = refs
    else:
        (h_ref, oa_ref, uv_ref, sg_ref, wa_ref, wb_ref, wo_ref, g2_ref, b2_ref, lg_ref, lb_ref,
         sw_ref, sb_ref, o_ref, vn_ref) = refs
    tm = h_ref.shape[0]
    u = uv_ref[:, 0:D_B]
    vn = _layer_norm(uv_ref[:, D_B:2 * D_B], lg_ref[...], lb_ref[...])
    if chunked:
        row = lax.broadcasted_iota(jnp.int32, (CHUNK, CHUNK), 0)
        col = lax.broadcasted_iota(jnp.int32, (CHUNK, CHUNK), 1)
        vnb = vn.astype(BF16)
        chunks = []
        for c in range(tm // CHUNK):
            parts = []
            for hg in range(N_GROUPS_B):
                w = jnp.where(col <= row, sw_ref[hg], 0.0).astype(BF16)
                parts.append(_dot(w, vnb[c * CHUNK:(c + 1) * CHUNK, hg * GROUP_CH:(hg + 1) * GROUP_CH]))
            chunks.append(jnp.concatenate(parts, axis=1) + sb_ref[...])
        s = jnp.concatenate(chunks, axis=0)
    else:
        s = vn * sw_ref[...] + sb_ref[...]
        vn_ref[...] = vn
    z = u * s
    branch_a = _dot(oa_ref[...], wa_ref[...])
    branch_b = _dot(z.astype(BF16), wb_ref[...])
    merged = (sg_ref[:, 0:D_MODEL].astype(F32) * branch_a
              + sg_ref[:, D_MODEL:2 * D_MODEL].astype(F32) * branch_b)
    y = ALPHA * h_ref[...] + _dot(merged.astype(BF16), wo_ref[...])
    o_ref[...] = _layer_norm(y, g2_ref[...], b2_ref[...])


def merge_ln(h, oa, uv, sgab, wa, wb, wo, g2, b2, lg, lb, sw, sb, *, tm, chunked):
    m, d = h.shape
    row = lambda n: pl.BlockSpec((tm, n), lambda i: (i, 0))
    full = lambda a: pl.BlockSpec(a.shape, lambda i: (0,) * a.ndim)
    vec = lambda a: a.reshape(1, -1)
    g2, b2, lg, lb = vec(g2), vec(b2), vec(lg), vec(lb)
    out_shape = [jax.ShapeDtypeStruct((m, d), F32)]
    out_specs = [row(d)]
    if not chunked:
        out_shape.append(jax.ShapeDtypeStruct((m, D_B), F32))
        out_specs.append(row(D_B))
    res = pl.pallas_call(
        functools.partial(_merge_ln_kernel, chunked=chunked),
        grid=(m // tm,),
        in_specs=[row(d), row(oa.shape[1]), row(2 * D_B), row(2 * D_MODEL), full(wa), full(wb), full(wo),
                  full(g2), full(b2), full(lg), full(lb), full(sw), full(sb)],
        out_specs=out_specs,
        out_shape=out_shape,
        compiler_params=_params(("parallel",)),
        name="merge_ln",
    )(h, oa, uv, sgab, wa, wb, wo, g2, b2, lg, lb, sw, sb)
    return res if not chunked else res[0]


def _group_mask():
    return (np.arange(N_HEADS)[:, None] // GROUP == np.arange(N_KV)[None, :]).astype(np.float32)


def _split_w_in(w_in):
    sizes = (Q_W,) + (KV_W,) * 6 + (GATE_W, D_B, D_B, D_MODEL, D_MODEL)
    cuts = np.cumsum(sizes)[:-1].tolist()
    q, kc, vc, ks, vs, kw, vw, g, u, v, ga, gb = jnp.split(w_in, cuts, axis=1)
    d = w_in.shape[0]
    gm = jnp.asarray(_group_mask())
    wq = (q.reshape(d, N_HEADS, 1, HEAD_DIM) * gm[None, :, :, None]).reshape(d, Q_EXP)
    wkv = jnp.concatenate([kc, vc, ks, vs, kw, vw], axis=1)
    wgt = jnp.pad(g, ((0, 0), (0, 128 - GATE_W)))
    wuv = jnp.concatenate([u, v], axis=1)
    wgab = jnp.concatenate([ga, gb], axis=1)
    return tuple(w.astype(BF16) for w in (wq, q * LOG2E, wkv, wgt, wuv, wgab))


def _expand_branch_a(w_a):
    gm = jnp.asarray(_group_mask())
    w = w_a.reshape(N_HEADS, 1, HEAD_DIM, w_a.shape[1]) * gm[:, :, None, None]
    return w.reshape(Q_EXP, w_a.shape[1]).astype(BF16)


def _cmp_weights(pe, w1, w2):
    n_sub = CMP_LEN // CMP_STRIDE
    w1r = w1.reshape(n_sub, CMP_STRIDE, HEAD_DIM, CMP_HIDDEN)
    eye = jnp.eye(N_KV, dtype=w1.dtype)
    w_ab = jnp.einsum('spdc,gh->pgdshc', w1r, eye).reshape(SUB_W, n_sub * N_KV * CMP_HIDDEN)
    w2x = jnp.einsum('cd,gh->gchd', w2, eye).reshape(N_KV * CMP_HIDDEN, SLAB)
    pe8 = jnp.pad(pe.reshape(1, -1), ((0, 7), (0, 0)))
    return w_ab.astype(BF16), pe8.astype(BF16), w1.astype(BF16), w2x.astype(BF16)


def _importance_matrix(n_cmp, n_cand):
    i = np.arange(n_cmp)[:, None]
    j = np.arange(n_cand)[None, :]
    a = (i >= CMP_RATIO * j - 1) & (i <= CMP_RATIO * j + CMP_RATIO - 1)
    return jnp.asarray(a.astype(np.float32)).astype(BF16)


def _compress_rows(rows, cw):
    w_ab, pe8, w1b, w2x = cw
    nb = rows.shape[0]
    ident = jnp.arange(nb, dtype=jnp.int32).reshape(nb, 1)
    return cmp_fin(cmp_ab(rows, ident, w_ab, pages_per_step=1), None, pe8, w1b, w2x)


def _compress_paged(pool, page_table, tail, cw, *, pages_per_step):
    w_ab, pe8, w1b, w2x = cw
    ab = cmp_ab_paged(pool, page_table, w_ab, pages_per_step=pages_per_step)
    one = jnp.zeros((1, 1), jnp.int32)
    ab_tail = cmp_ab(tail.reshape(1, -1, SUB_W), one, w_ab, pages_per_step=1)
    ab_tail = ab_tail.reshape(tail.shape[0], tail.shape[1], -1)
    return cmp_fin(ab, ab_tail, pe8, w1b, w2x)


def _position_minor(a):
    n, p = a.shape[0], a.shape[1]
    return jnp.transpose(a, (0, 2, 3, 1)).reshape(n, SLAB, p)


def _position_major(a_t):
    n, _, p = a_t.shape
    return jnp.transpose(a_t.reshape(n, N_KV, HEAD_DIM, p), (0, 3, 1, 2))


def kernel(x_prompt, x_sample, cache_cmp_k, cache_cmp_v, cache_sel_k, cache_sel_v, cache_win_k,
           cache_win_v, page_table, ffn1_w_in, ffn1_w_out, ln1_g, ln1_b, w_in, cmp_pe_k, cmp_w1_k,
           cmp_w2_k, cmp_pe_v, cmp_w1_v, cmp_w2_v, gmlp_ln_g, gmlp_ln_b, spatial_w, spatial_b,
           w_branch_a, w_branch_b, w_out, ln2_g, ln2_b, ffn2_w_in, ffn2_w_out, ln3_g, ln3_b):
    nb, t, d = x_prompt.shape
    db, n_new, _ = x_sample.shape
    assert n_new == 1 and t % SEL_TILE == 0
    n_pages = page_table.shape[1]
    past = n_pages * PAGE_SIZE
    n_phys = cache_cmp_k.shape[0]
    m = nb * t

    f1_in, f1_out = ffn1_w_in.astype(BF16), ffn1_w_out.astype(BF16)
    f2_in, f2_out = ffn2_w_in.astype(BF16), ffn2_w_out.astype(BF16)
    wq, wq_base2, *proj_rest = _split_w_in(w_in)
    wa = _expand_branch_a(w_branch_a)
    wb = w_branch_b.astype(BF16)
    wo = w_out.astype(BF16)
    cw_k = _cmp_weights(cmp_pe_k, cmp_w1_k, cmp_w2_k)
    cw_v = _cmp_weights(cmp_pe_v, cmp_w1_v, cmp_w2_v)

    h = ffn_ln(x_prompt.reshape(m, d), f1_in, f1_out, ln1_g, ln1_b, tm=1024)
    (qt, kct, vct, kst, vst, kwt, vwt, kcb, vcb, ksb, kwb, vstb, vwtb, gtt, uv,
     sgab) = mixer_proj_t(h, wq_base2, *proj_rest, nb=nb, tm=512)
    n_cmp = t // CMP_STRIDE
    kcc = _compress_rows(kcb.reshape(nb, n_cmp, SUB_W), cw_k)
    vcc = _compress_rows(vcb.reshape(nb, n_cmp, SUB_W), cw_v)
    imp_t = _importance_matrix(n_cmp, t // SEL_BLOCK).T
    o_a = nsa_prompt(qt, gtt, kcc, jnp.swapaxes(vcc, 1, 2), ksb.reshape(nb, t, SLAB), vstb,
                     kwb.reshape(nb, t, SLAB), vwtb, imp_t)
    sb_tile = jnp.repeat(spatial_b.T, GROUP_CH, axis=1)
    h2 = merge_ln(h, o_a.reshape(m, Q_W), uv, sgab, w_branch_a.astype(BF16), wb, wo, ln2_g, ln2_b,
                  gmlp_ln_g, gmlp_ln_b, spatial_w, sb_tile, tm=512, chunked=True)
    y_prompt = ffn_ln(h2, f2_in, f2_out, ln3_g, ln3_b, tm=1024).reshape(nb, t, d)
    wkeep = min(WINDOW, t)
    p_outs = tuple(_position_major(a) for a in
                   (kct, vct, kst, vst, kwt[:, :, t - wkeep:], vwt[:, :, t - wkeep:]))

    hs = ffn_ln(x_sample.reshape(db, d), f1_in, f1_out, ln1_g, ln1_b, tm=db)
    (qs, kcs, vcs, kss, vss, kws, vws, _, gts, uvs, sgabs) = mixer_proj(hs, wq, *proj_rest, tm=db)
    t_pad = -(-(past + n_new) // SEL_BLOCK) * SEL_BLOCK
    n_tail = (t_pad - past) // CMP_STRIDE
    tail_rows = 16

    def tail_of(new):
        flat = jnp.pad(new, ((0, 0), (0, tail_rows * SUB_W - SLAB)))
        return flat.reshape(db, tail_rows, SUB_W)

    assert n_tail <= tail_rows
    pps = min(64, n_pages)
    kcc_s = _compress_paged(_position_minor(cache_cmp_k), page_table, tail_of(kcs), cw_k,
                            pages_per_step=pps)
    vcc_s = _compress_paged(_position_minor(cache_cmp_v), page_table, tail_of(vcs), cw_v,
                            pages_per_step=pps)
    n_sel_s = t_pad // SEL_BLOCK
    n_cand = -(-n_sel_s // 128) * 128
    imp_s = _importance_matrix(kcc_s.shape[1], n_cand)
    q8 = qs.reshape(db, N_HEADS, SLAB)
    o_cmp_s, idx_s = nsa_sample_cmp(q8, kcc_s, vcc_s, imp_s, qpos=past, n_sel=n_sel_s)
    idx_flat = idx_s[:, :N_KV, :SEL_TOP].reshape(db, N_KV * SEL_TOP)
    gt8 = jnp.pad(gts[:, :GATE_W].reshape(db, N_HEADS, 3), ((0, 0), (0, 0), (0, 125)))
    row3 = lambda a: a.reshape(db, 1, SLAB)
    win3 = lambda a: a.reshape(db, -1, SLAB)
    o_as = nsa_sample_sel(page_table, idx_flat, _position_minor(cache_sel_k),
                          _position_minor(cache_sel_v),
                          q8, gt8, o_cmp_s, row3(kss), row3(vss), win3(cache_win_k),
                          win3(cache_win_v), row3(kws), row3(vws), qpos=past)
    sw_row = jnp.repeat(spatial_w[:, 0, 0], GROUP_CH).reshape(1, D_B)
    sb_row = jnp.repeat(spatial_b[:, 0], GROUP_CH).reshape(1, D_B)
    hs2, vn_s = merge_ln(hs, o_as.reshape(db, Q_EXP), uvs, sgabs, wa, wb, wo, ln2_g, ln2_b,
                         gmlp_ln_g, gmlp_ln_b, sw_row, sb_row, tm=db, chunked=False)
    y_sample = ffn_ln(hs2, f2_in, f2_out, ln3_g, ln3_b, tm=db).reshape(db, n_new, d)
    kv4s = lambda a: a.reshape(db, n_new, N_KV, HEAD_DIM)
    wbuf = cache_win_k.shape[1]
    s_win_k = jnp.concatenate([cache_win_k, kv4s(kws)], axis=1)[:, -wbuf:]
    s_win_v = jnp.concatenate([cache_win_v, kv4s(vws)], axis=1)[:, -wbuf:]

    return (y_prompt, y_sample) + p_outs + (kv4s(kcs), kv4s(vcs), kv4s(kss), kv4s(vss),
                                            s_win_k, s_win_v, vn_s.reshape(db, n_new, D_B))
```
